```python
import jax, jax.numpy as jnp
from jax import lax
import numpy as np

D_MODEL = 2048
BATCH = 2
SEQ = 16384
DEPTH = 2

CTX_LEN = 256
GRID_W = 64
N_MIXERS = 2
N_A = (DEPTH + N_MIXERS - 1) // N_MIXERS
N_B = DEPTH // N_MIXERS
EPS = 1e-6
D_RNN = D_MODEL
LRU_HEADS = 16
LRU_HEAD_DIM = D_RNN // LRU_HEADS
LRU_CONV_W = 4
LRU_CONV_PAD_LEFT = 2
LRU_C = 8.0
D_SC = D_MODEL
SC_CONV_W = 3
SC_CONV_PAD_LEFT = 1
PEER_HEADS = 8
PEER_N_KEYS = 128
PEER_EXPERTS = PEER_N_KEYS * PEER_N_KEYS
PEER_TOPK = 16
PEER_D_QUERY = 256
PEER_D_KEY = PEER_D_QUERY // 2
PEER_BLOCK = 128

kernel_name = 'hybrid_rglru_shortconv_peer_dit'


def rms_norm(x, g):
    xf = x.astype(jnp.float32)
    y = xf * lax.rsqrt(jnp.mean(xf * xf, axis=-1, keepdims=True) + EPS)
    return (y * g.astype(jnp.float32)).astype(x.dtype)


def modulate(h, shift, scale):
    return h * (1 + scale) + shift


def depthwise_conv(x, w, b, pad_left):
    k_w, length = w.shape[0], x.shape[-2]
    pad = [(0, 0)] * (x.ndim - 2) + [(pad_left, k_w - 1 - pad_left), (0, 0)]
    xp = jnp.pad(x, pad)
    y = b + w[0] * xp[..., 0:length, :]
    for k in range(1, k_w):
        y = y + w[k] * xp[..., k:k + length, :]
    return y


def grid_conv(x, w, b, pad_left):
    bsz, length, ch = x.shape
    rows = length // GRID_W
    y = depthwise_conv(x.reshape(bsz, rows, GRID_W, ch), w, b, pad_left)
    return y.reshape(bsz, length, ch)


def rglru_coeffs(xc, w_a, b_a, w_x, b_x, lam):
    bsz, length, ch = xc.shape
    xh = xc.reshape(bsz, length, LRU_HEADS, LRU_HEAD_DIM)
    r = jax.nn.sigmoid(jnp.einsum('blhd,hde->blhe', xh, w_a).reshape(bsz, length, ch) + b_a)
    i = jax.nn.sigmoid(jnp.einsum('blhd,hde->blhe', xh, w_x).reshape(bsz, length, ch) + b_x)
    log_a = -LRU_C * r.astype(jnp.float32) * jax.nn.softplus(-lam.astype(jnp.float32))
    a = jnp.exp(log_a)
    b = jnp.sqrt(-jnp.expm1(2.0 * log_a)) * (i * xc).astype(jnp.float32)
    return a, b


def _combine(e1, e2):
    a1, b1 = e1
    a2, b2 = e2
    return a1 * a2, a2 * b1 + b2


def linear_scan(a, b, h0, reverse):
    if h0 is not None:
        edge = -1 if reverse else 0
        b = b.at[:, edge].add(a[:, edge] * h0)
    _, h = lax.associative_scan(_combine, (a, b), reverse=reverse, axis=1)
    return h


def rglru_mixer(h_lat, h_ctx, w_in, conv_w, conv_b, w_a, b_a, w_x, b_x, lam, w_out):
    gate, xb = jnp.split(h_lat @ w_in, 2, axis=-1)
    xc = grid_conv(xb, conv_w, conv_b, LRU_CONV_PAD_LEFT)
    xc_ctx = depthwise_conv(h_ctx @ w_in[:, D_RNN:], conv_w, conv_b, LRU_CONV_PAD_LEFT)
    ys = []
    for d, reverse in enumerate((False, True)):
        a_c, b_c = rglru_coeffs(xc_ctx, w_a[d], b_a[d], w_x[d], b_x[d], lam[d])
        h_c = linear_scan(a_c, b_c, None, reverse)
        h0 = h_c[:, 0] if reverse else h_c[:, -1]
        a_l, b_l = rglru_coeffs(xc, w_a[d], b_a[d], w_x[d], b_x[d], lam[d])
        ys.append(linear_scan(a_l, b_l, h0, reverse))
    y = (ys[0] + ys[1]).astype(h_lat.dtype)
    return (jax.nn.gelu(gate) * y) @ w_out


def shortconv_mixer(h_lat, w_in, conv_w, conv_b, w_out):
    bg, cg, v = jnp.split(h_lat @ w_in, 3, axis=-1)
    return (bg * grid_conv(cg * v, conv_w, conv_b, SC_CONV_PAD_LEFT)) @ w_out


def peer_ffn(h, w_q, sub_keys, u, v):
    bsz, length, dm = h.shape
    blocks = h.reshape(-1, PEER_BLOCK, dm)

    def block(xt):
        q = (xt @ w_q).reshape(PEER_BLOCK, PEER_HEADS, 2, PEER_D_KEY)
        s = jnp.einsum('thpd,hpkd->thpk', q, sub_keys)
        sv, si = lax.top_k(s, PEER_TOPK)
        cand = sv[:, :, 0, :, None] + sv[:, :, 1, None, :]
        cv, ci = lax.top_k(cand.reshape(PEER_BLOCK, PEER_HEADS, PEER_TOPK * PEER_TOPK), PEER_TOPK)
        i1 = jnp.take_along_axis(si[:, :, 0], ci // PEER_TOPK, axis=-1)
        i2 = jnp.take_along_axis(si[:, :, 1], ci % PEER_TOPK, axis=-1)
        expert = i1 * PEER_N_KEYS + i2
        g = jax.nn.softmax(cv.astype(jnp.float32), axis=-1)
        ue = jnp.take(u, expert, axis=0)
        ve = jnp.take(v, expert, axis=0)
        act = jax.nn.gelu(jnp.einsum('thkd,td->thk', ue, xt).astype(jnp.float32))
        return jnp.einsum('thk,thkd->td', (g * act).astype(xt.dtype), ve)

    return lax.map(block, blocks).reshape(bsz, length, dm)


def setup_inputs(seed: int = 0) -> dict:
    key = jax.random.key(seed)
    ks = jax.random.split(key, 26)

    def nrm(k, shape, scale):
        return scale * jax.random.normal(k, shape, jnp.float32)

    a0 = jax.random.uniform(ks[16], (N_A, 2, D_RNN), jnp.float32, 0.9, 0.999)
    s0 = a0 ** (1.0 / LRU_C)
    lru_lambda = jnp.log(s0) - jnp.log1p(-s0)
    return {
        'x': nrm(ks[0], (BATCH, SEQ, D_MODEL), 1.0),
        'c': nrm(ks[1], (BATCH, D_MODEL), 1.0),
        'ctx': nrm(ks[2], (BATCH, CTX_LEN, D_MODEL), 1.0),
        'c_ctx': nrm(ks[3], (D_MODEL,), 1.0),
        'w_mod': nrm(ks[4], (DEPTH, D_MODEL, 6 * D_MODEL), 0.5 * D_MODEL ** -0.5),
        'b_mod': nrm(ks[5], (DEPTH, 6 * D_MODEL), 0.02),
        'norm_mix_g': 1.0 + nrm(ks[6], (DEPTH, D_MODEL), 0.02),
        'norm_ffn_g': 1.0 + nrm(ks[7], (DEPTH, D_MODEL), 0.02),
        'norm_final_g': 1.0 + nrm(ks[8], (D_MODEL,), 0.02),
        'lru_w_in': nrm(ks[9], (N_A, D_MODEL, 2 * D_RNN), D_MODEL ** -0.5),
        'lru_conv_w': nrm(ks[10], (N_A, LRU_CONV_W, D_RNN), LRU_CONV_W ** -0.5),
        'lru_conv_b': nrm(ks[11], (N_A, D_RNN), 0.02),
        'lru_w_a': nrm(ks[12], (N_A, 2, LRU_HEADS, LRU_HEAD_DIM, LRU_HEAD_DIM), LRU_HEAD_DIM ** -0.5),
        'lru_b_a': nrm(ks[13], (N_A, 2, D_RNN), 0.02),
        'lru_w_x': nrm(ks[14], (N_A, 2, LRU_HEADS, LRU_HEAD_DIM, LRU_HEAD_DIM), LRU_HEAD_DIM ** -0.5),
        'lru_b_x': nrm(ks[15], (N_A, 2, D_RNN), 0.02),
        'lru_lambda': lru_lambda,
        'lru_w_out': nrm(ks[17], (N_A, D_RNN, D_MODEL), D_RNN ** -0.5),
        'sc_w_in': nrm(ks[18], (N_B, D_MODEL, 3 * D_SC), D_MODEL ** -0.5),
        'sc_conv_w': nrm(ks[19], (N_B, SC_CONV_W, D_SC), SC_CONV_W ** -0.5),
        'sc_conv_b': nrm(ks[20], (N_B, D_SC), 0.02),
        'sc_w_out': nrm(ks[21], (N_B, D_SC, D_MODEL), D_SC ** -0.5),
        'peer_w_q': nrm(ks[22], (DEPTH, D_MODEL, PEER_HEADS * PEER_D_QUERY), D_MODEL ** -0.5),
        'peer_sub_keys': nrm(ks[23], (DEPTH, PEER_HEADS, 2, PEER_N_KEYS, PEER_D_KEY), PEER_D_KEY ** -0.5),
        'peer_u': nrm(ks[24], (DEPTH, PEER_EXPERTS, D_MODEL), D_MODEL ** -0.5),
        'peer_v': nrm(ks[25], (DEPTH, PEER_EXPERTS, D_MODEL), PEER_HEADS ** -0.5),
    }


def reference(x, c, ctx, c_ctx, w_mod, b_mod, norm_mix_g, norm_ffn_g, norm_final_g,
              lru_w_in, lru_conv_w, lru_conv_b, lru_w_a, lru_b_a, lru_w_x, lru_b_x, lru_lambda, lru_w_out,
              sc_w_in, sc_conv_w, sc_conv_b, sc_w_out,
              peer_w_q, peer_sub_keys, peer_u, peer_v):
    silu_c = jax.nn.silu(c)
    silu_c_ctx = jax.nn.silu(c_ctx)
    for i in range(DEPTH):
        mixer, j = i % N_MIXERS, i // N_MIXERS
        mod = (silu_c @ w_mod[i] + b_mod[i])[:, None, :]
        sh1, sc1, g1, sh2, sc2, g2 = jnp.split(mod, 6, axis=-1)
        h = modulate(rms_norm(x, norm_mix_g[i]), sh1, sc1)
        if mixer == 0:
            mod_c = silu_c_ctx @ w_mod[i] + b_mod[i]
            h_ctx = modulate(rms_norm(ctx, norm_mix_g[i]), mod_c[:D_MODEL], mod_c[D_MODEL:2 * D_MODEL])
            out = rglru_mixer(h, h_ctx, lru_w_in[j], lru_conv_w[j], lru_conv_b[j], lru_w_a[j], lru_b_a[j],
                              lru_w_x[j], lru_b_x[j], lru_lambda[j], lru_w_out[j])
        else:
            out = shortconv_mixer(h, sc_w_in[j], sc_conv_w[j], sc_conv_b[j], sc_w_out[j])
        x = x + g1 * out
        h = modulate(rms_norm(x, norm_ffn_g[i]), sh2, sc2)
        x = x + g2 * peer_ffn(h, peer_w_q[i], peer_sub_keys[i], peer_u[i], peer_v[i])
    return rms_norm(x, norm_final_g)
```

```python
import functools

import jax
import jax.numpy as jnp
from jax import lax
from jax.experimental import pallas as pl
from jax.experimental.pallas import tpu as pltpu

EPS = 1e-6
GRID_W = 64
LRU_HEADS = 16
LRU_C = 8.0
PEER_HEADS = 8
PEER_N_KEYS = 128
PEER_TOPK = 16
PEER_PER_TOKEN = PEER_HEADS * PEER_TOPK

LANES = 128
SUBLANES = 8
VMEM_LIMIT = 48 * 1024 * 1024

F32 = jnp.float32
BF16 = jnp.bfloat16
HIGHEST = lax.Precision.HIGHEST


def _params(sem):
    return pltpu.CompilerParams(dimension_semantics=sem, vmem_limit_bytes=VMEM_LIMIT)


def _gelu_tanh(x):
    c = 0.7978845608028654
    return x * (0.5 * (1.0 + jnp.tanh(c * (x + 0.044715 * (x * x * x)))))


def _neg_expm1(z, exp_z):
    poly = z * (1.0 + z * (1 / 2 + z * (1 / 6 + z * (1 / 24 + z * (1 / 120 + z * (1 / 720 + z * (1 / 5040)))))))
    return jnp.where(z > -0.25, -poly, 1.0 - exp_z)


def _mod_kernel(c_ref, w_ref, b_ref, o_ref):
    c = c_ref[...]
    s = c * jax.nn.sigmoid(c)
    o_ref[...] = jnp.dot(s, w_ref[...], preferred_element_type=F32, precision=HIGHEST) + b_ref[...]


def _mod_vectors(c_rows, w, b):
    d, n = w.shape
    tn = 1024
    return pl.pallas_call(
        _mod_kernel,
        grid=(n // tn,),
        in_specs=[pl.BlockSpec((SUBLANES, d), lambda j: (0, 0)),
                  pl.BlockSpec((d, tn), lambda j: (0, j)),
                  pl.BlockSpec((1, tn), lambda j: (0, j))],
        out_specs=pl.BlockSpec((SUBLANES, tn), lambda j: (0, j)),
        out_shape=jax.ShapeDtypeStruct((SUBLANES, n), F32),
        compiler_params=_params(("arbitrary",)),
        name="mod_vectors",
    )(c_rows, w, b.reshape(1, n))


def _nmm_kernel(emit_h, x_ref, g_ref, sh_ref, sc_ref, w_ref, o_ref, *rest):
    if emit_h:
        h_ref, hn_ref = rest
    else:
        (hn_ref,) = rest

    @pl.when(pl.program_id(1) == 0)
    def _():
        x = x_ref[...]
        ms = jnp.mean(x * x, axis=-1, keepdims=True)
        y = (x * lax.rsqrt(ms + EPS)) * g_ref[...]
        h = y * (1.0 + sc_ref[0]) + sh_ref[0]
        hn_ref[...] = h.astype(BF16)
        if emit_h:
            h_ref[...] = h

    o_ref[...] = jnp.dot(hn_ref[...], w_ref[...], preferred_element_type=F32)


def _norm_mod_matmul(x, gamma, shift, scale, w_bf16, rows_per_batch, tm, emit_h=False):
    t, d = x.shape
    n = w_bf16.shape[1]
    tn = 1024
    bpb = rows_per_batch // tm
    out_shape = [jax.ShapeDtypeStruct((t, n), F32)]
    out_specs = [pl.BlockSpec((tm, tn), lambda i, j: (i, j))]
    if emit_h:
        out_shape.append(jax.ShapeDtypeStruct((t, d), F32))
        out_specs.append(pl.BlockSpec((tm, d), lambda i, j: (i, 0)))
    res = pl.pallas_call(
        functools.partial(_nmm_kernel, emit_h),
        grid=(t // tm, n // tn),
        in_specs=[pl.BlockSpec((tm, d), lambda i, j: (i, 0)),
                  pl.BlockSpec((1, d), lambda i, j: (0, 0)),
                  pl.BlockSpec((1, 1, d), lambda i, j: (i // bpb, 0, 0)),
                  pl.BlockSpec((1, 1, d), lambda i, j: (i // bpb, 0, 0)),
                  pl.BlockSpec((d, tn), lambda i, j: (0, j))],
        out_specs=out_specs,
        out_shape=out_shape,
        scratch_shapes=[pltpu.VMEM((tm, d), BF16)],
        compiler_params=_params(("arbitrary", "arbitrary")),
        name="norm_mod_matmul",
    )(x, gamma.reshape(1, d), shift, scale, w_bf16)
    return res if emit_h else res[0]


def _row_conv(u, taps, bias, pad_left, row_len):
    tm = u.shape[0]
    pos = lax.broadcasted_iota(jnp.int32, (tm, 1), 0) % row_len
    y = None
    for k in range(len(taps)):
        off = k - pad_left
        if off == 0:
            term = u
        else:
            rolled = pltpu.roll(u, (-off) % tm, axis=0)
            valid = jnp.logical_and(pos + off >= 0, pos + off < row_len)
            term = jnp.where(valid, rolled, 0.0)
        y = (bias + taps[k] * term) if y is None else (y + taps[k] * term)
    return y


def _lru_coef_kernel(row_len, xb_ref, cw_ref, cb_ref, wa_ref, ba_ref, wx_ref, bx_ref, lam_ref,
                     af_ref, bf_ref, ar_ref, br_ref):
    taps = [cw_ref[k:k + 1, :] for k in range(cw_ref.shape[0])]
    xc = _row_conv(xb_ref[...], taps, cb_ref[...], 2, row_len)
    d = xc.shape[1]
    hd = d // LRU_HEADS
    outs = ((af_ref, bf_ref), (ar_ref, br_ref))
    for dr in range(2):
        lam = lam_ref[dr:dr + 1, :]
        z = -lam
        sp = jnp.maximum(z, 0.0) + jnp.log1p(jnp.exp(-jnp.abs(z)))
        a_ref, b_ref = outs[dr]
        for h in range(LRU_HEADS):
            sl = slice(h * hd, (h + 1) * hd)
            xh = xc[:, sl]
            xh16 = xh.astype(BF16)
            r = jax.nn.sigmoid(jnp.dot(xh16, wa_ref[dr, h], preferred_element_type=F32) + ba_ref[dr:dr + 1, sl])
            i = jax.nn.sigmoid(jnp.dot(xh16, wx_ref[dr, h], preferred_element_type=F32) + bx_ref[dr:dr + 1, sl])
            log_a = (-LRU_C) * r * sp[:, sl]
            a = jnp.exp(log_a)
            a_ref[:, sl] = a
            b_ref[:, sl] = jnp.sqrt(_neg_expm1(2.0 * log_a, a * a)) * (i * xh)


def _lru_coeffs(z, col_block, conv_w, conv_b, w_a16, b_a, w_x16, b_x, lam, row_len, tm):
    t = z.shape[0]
    d = conv_w.shape[1]
    full = lambda *s: pl.BlockSpec(s, lambda i: (0,) * len(s))
    o_spec = pl.BlockSpec((tm, d), lambda i: (i, 0))
    o_shape = jax.ShapeDtypeStruct((t, d), F32)
    return pl.pallas_call(
        functools.partial(_lru_coef_kernel, row_len),
        grid=(t // tm,),
        in_specs=[pl.BlockSpec((tm, d), lambda i: (i, col_block)),
                  full(*conv_w.shape), full(1, d),
                  full(*w_a16.shape), full(2, d), full(*w_x16.shape), full(2, d), full(2, d)],
        out_specs=[o_spec] * 4,
        out_shape=[o_shape] * 4,
        compiler_params=_params(("arbitrary",)),
        name="lru_coeffs",
    )(z, conv_w, conv_b.reshape(1, d), w_a16, b_a, w_x16, b_x, lam)


def _scan_kernel(reverse, add_prev, a_ref, b_ref, h0_ref, *rest):
    if add_prev:
        y_ref, o_ref, st_ref = rest
    else:
        o_ref, st_ref = rest
    ts, cw = a_ref.shape
    ng = ts // SUBLANES

    @pl.when(pl.program_id(2) == 0)
    def _():
        st_ref[...] = jnp.broadcast_to(h0_ref[...], st_ref.shape)

    row = lax.broadcasted_iota(jnp.int32, (SUBLANES, cw), 0)

    def body(gi, h):
        g = (ng - 1 - gi) if reverse else gi
        off = pl.multiple_of(g * SUBLANES, SUBLANES)
        a = a_ref[pl.ds(off, SUBLANES), :]
        b = b_ref[pl.ds(off, SUBLANES), :]
        for s in (1, 2, 4):
            if reverse:
                shift, m = SUBLANES - s, row < SUBLANES - s
            else:
                shift, m = s, row >= s
            a_s = jnp.where(m, pltpu.roll(a, shift, axis=0), 1.0)
            b_s = jnp.where(m, pltpu.roll(b, shift, axis=0), 0.0)
            b = a * b_s + b
            a = a * a_s
        hh = a * h + b
        if add_prev:
            o_ref[pl.ds(off, SUBLANES), :] = hh + y_ref[pl.ds(off, SUBLANES), :]
        else:
            o_ref[pl.ds(off, SUBLANES), :] = hh
        last = hh[0:1, :] if reverse else hh[SUBLANES - 1:SUBLANES, :]
        return jnp.broadcast_to(last, (SUBLANES, cw))

    st_ref[...] = lax.fori_loop(0, ng, body, st_ref[...], unroll=2)


def _linear_scan(a, b, h0, reverse, prev=None, ts=512, cw=512):
    bsz, s, c = a.shape
    ts = min(ts, s)
    ns = s // ts
    tmap = (lambda bb, cc, j: (bb, ns - 1 - j, cc)) if reverse else (lambda bb, cc, j: (bb, j, cc))
    blk = pl.BlockSpec((None, ts, cw), tmap)
    ins = [a, b, h0]
    in_specs = [blk, blk, pl.BlockSpec((None, 1, cw), lambda bb, cc, j: (bb, 0, cc))]
    if prev is not None:
        ins.append(prev)
        in_specs.append(blk)
    return pl.pallas_call(
        functools.partial(_scan_kernel, reverse, prev is not None),
        grid=(bsz, c // cw, ns),
        in_specs=in_specs,
        out_specs=blk,
        out_shape=jax.ShapeDtypeStruct((bsz, s, c), F32),
        scratch_shapes=[pltpu.VMEM((SUBLANES, cw), F32)],
        compiler_params=_params(("arbitrary", "arbitrary", "arbitrary")),
        name="linear_scan",
    )(*ins)


def _lru_out_kernel(gate_ref, y_ref, x_ref, g1_ref, w_ref, o_ref):
    p = _gelu_tanh(gate_ref[...]) * y_ref[...]
    out = jnp.dot(p.astype(BF16), w_ref[...], preferred_element_type=F32)
    o_ref[...] = x_ref[...] + g1_ref[0] * out


def _lru_out(z, y, x, g1, w16, rows_per_batch, tm=256):
    t, d = x.shape
    bpb = rows_per_batch // tm
    row = pl.BlockSpec((tm, d), lambda i: (i, 0))
    return pl.pallas_call(
        _lru_out_kernel,
        grid=(t // tm,),
        in_specs=[row, row, row,
                  pl.BlockSpec((1, 1, d), lambda i: (i // bpb, 0, 0)),
                  pl.BlockSpec((d, d), lambda i: (0, 0))],
        out_specs=row,
        out_shape=jax.ShapeDtypeStruct((t, d), F32),
        compiler_params=_params(("arbitrary",)),
        name="lru_out",
    )(z, y, x, g1, w16)


def _sc_out_kernel(bg_ref, cg_ref, v_ref, cw_ref, cb_ref, x_ref, g1_ref, w_ref, o_ref):
    taps = [cw_ref[k:k + 1, :] for k in range(cw_ref.shape[0])]
    y = _row_conv(cg_ref[...] * v_ref[...], taps, cb_ref[...], 1, GRID_W)
    p = bg_ref[...] * y
    out = jnp.dot(p.astype(BF16), w_ref[...], preferred_element_type=F32)
    o_ref[...] = x_ref[...] + g1_ref[0] * out


def _sc_out(z, conv_w, conv_b, x, g1, w16, rows_per_batch, tm=256):
    t, d = x.shape
    bpb = rows_per_batch // tm
    row = pl.BlockSpec((tm, d), lambda i: (i, 0))
    col = lambda c: pl.BlockSpec((tm, d), lambda i: (i, c))
    return pl.pallas_call(
        _sc_out_kernel,
        grid=(t // tm,),
        in_specs=[col(0), col(1), col(2),
                  pl.BlockSpec(conv_w.shape, lambda i: (0, 0)),
                  pl.BlockSpec((1, d), lambda i: (0, 0)),
                  row,
                  pl.BlockSpec((1, 1, d), lambda i: (i // bpb, 0, 0)),
                  pl.BlockSpec((d, d), lambda i: (0, 0))],
        out_specs=row,
        out_shape=jax.ShapeDtypeStruct((t, d), F32),
        compiler_params=_params(("arbitrary",)),
        name="sc_out",
    )(z, z, z, conv_w, conv_b.reshape(1, d), x, g1, w16)


def _top_rows(vals, idx_f, fill, n_out, emit):
    for r in range(n_out):
        m = jnp.max(vals, axis=0, keepdims=True)
        am = jnp.min(jnp.where(vals == m, idx_f, fill), axis=0, keepdims=True)
        emit(r, m, am)
        vals = jnp.where(idx_f == am, -jnp.inf, vals)


def _route_kernel(q_ref, k_ref, ids_ref, gate_ref, sv_ref, si_ref, cv_ref):
    tb = q_ref.shape[0]
    nk, dk = k_ref.shape[1], k_ref.shape[2]
    kk = PEER_TOPK
    key_idx = lax.broadcasted_iota(jnp.int32, (nk, tb), 0).astype(F32)
    for p in range(2):
        s_t = lax.dot_general(k_ref[p], q_ref[:, p * dk:(p + 1) * dk], (((1,), (1,)), ((), ())),
                              precision=HIGHEST, preferred_element_type=F32)

        def emit(r, m, am, p=p):
            sv_ref[p, r:r + 1, :] = m
            si_ref[p, r:r + 1, :] = am

        _top_rows(s_t, key_idx, float(nk), kk, emit)

    sv0, sv1 = sv_ref[0], sv_ref[1]
    si0, si1 = si_ref[0], si_ref[1]
    cand = jnp.concatenate([sv0[i:i + 1, :] + sv1 for i in range(kk)], axis=0)
    eid = jnp.concatenate([si0[i:i + 1, :] * float(nk) + si1 for i in range(kk)], axis=0)
    cand_idx = lax.broadcasted_iota(jnp.int32, (kk * kk, tb), 0).astype(F32)

    def emit2(r, m, am):
        e = jnp.max(jnp.where(cand_idx == am, eid, -1.0), axis=0, keepdims=True)
        cv_ref[r:r + 1, :] = m
        ids_ref[r:r + 1, :] = e.astype(jnp.int32)

    _top_rows(cand, cand_idx, float(kk * kk), kk, emit2)
    cv = cv_ref[...]
    ex = jnp.exp(cv - jnp.max(cv, axis=0, keepdims=True))
    gate_ref[...] = ex / jnp.sum(ex, axis=0, keepdims=True)


def _peer_route(q, sub_keys, tb=256):
    t = q.shape[0]
    nh, _, nk, dk = sub_keys.shape
    kk = PEER_TOPK
    o_spec = pl.BlockSpec((kk, tb), lambda i, h: (h, i))
    return pl.pallas_call(
        _route_kernel,
        grid=(t // tb, nh),
        in_specs=[pl.BlockSpec((tb, 2 * dk), lambda i, h: (i, h)),
                  pl.BlockSpec((None, 2, nk, dk), lambda i, h: (h, 0, 0, 0))],
        out_specs=[o_spec, o_spec],
        out_shape=[jax.ShapeDtypeStruct((nh * kk, t), jnp.int32),
                   jax.ShapeDtypeStruct((nh * kk, t), F32)],
        scratch_shapes=[pltpu.VMEM((2, kk, tb), F32), pltpu.VMEM((2, kk, tb), F32), pltpu.VMEM((kk, tb), F32)],
        compiler_params=_params(("arbitrary", "arbitrary")),
        name="peer_route",
    )(q, sub_keys)


def _pack_kernel(u_ref, v_ref, o_ref):
    ub = pltpu.bitcast(u_ref[...].astype(BF16).astype(F32), jnp.uint32)
    vb = pltpu.bitcast(v_ref[...].astype(BF16).astype(F32), jnp.uint32)
    o_ref[...] = ub | (vb >> 16)


def _pack_experts(u, v, te=512):
    e, d = u.shape
    blk = pl.BlockSpec((te, d), lambda i: (i, 0))
    return pl.pallas_call(
        _pack_kernel,
        grid=(e // te,),
        in_specs=[blk, blk],
        out_specs=blk,
        out_shape=jax.ShapeDtypeStruct((e, d), jnp.uint32),
        compiler_params=_params(("arbitrary",)),
        name="pack_experts",
    )(u, v)


PEER_TOKENS_PER_STEP = 128
PEER_SLOTS = 4


def _peer_kernel(ids_ref, gate_ref, h_ref, x_ref, g2_ref, tab_ref, o_ref, buf_ref, sem_ref):
    tb, _, d = h_ref.shape
    ne = PEER_PER_TOKEN
    nslots = buf_ref.shape[0]
    nchunk = d // LANES

    def row_copy(tok, k, slot):
        return pltpu.make_async_copy(tab_ref.at[pl.ds(ids_ref[tok, k], 1), :],
                                     buf_ref.at[slot, pl.ds(k, 1), :], sem_ref.at[slot])

    def start_token(tok, slot):
        for k in range(ne):
            row_copy(tok, k, slot).start()

    def wait_token(slot):
        pltpu.make_async_copy(tab_ref.at[pl.ds(0, ne), :], buf_ref.at[slot], sem_ref.at[slot]).wait()

    for s in range(nslots - 1):
        start_token(s, s)

    lane_tok = lax.broadcasted_iota(jnp.int32, (ne, tb), 1)

    def body(tok, carry):
        slot = tok % nslots
        nxt = tok + nslots - 1

        @pl.when(nxt < tb)
        def _():
            start_token(nxt, nxt % nslots)

        wait_token(slot)
        xrow = h_ref[tok]
        acc = jnp.zeros((ne, LANES), F32)
        for c in range(nchunk):
            w32 = buf_ref[slot, :, c * LANES:(c + 1) * LANES]
            u = pltpu.bitcast(w32 & jnp.uint32(0xFFFF0000), F32)
            acc = acc + u * xrow[:, c * LANES:(c + 1) * LANES]
        act = _gelu_tanh(jnp.sum(acc, axis=1, keepdims=True))
        gcol = jnp.sum(jnp.where(lane_tok == tok, gate_ref[...], 0.0), axis=1, keepdims=True)
        wb = jnp.broadcast_to(gcol * act, (ne, LANES))
        for c in range(nchunk):
            w32 = buf_ref[slot, :, c * LANES:(c + 1) * LANES]
            v = pltpu.bitcast(w32 << 16, F32)
            oc = jnp.sum(v * wb, axis=0, keepdims=True)
            sl = slice(c * LANES, (c + 1) * LANES)
            o_ref[tok, :, sl] = x_ref[tok, :, sl] + g2_ref[0][:, sl] * oc
        return carry

    lax.fori_loop(0, tb, body, 0)


def _peer_experts(ids_t, gates, h, x, g2, table, rows_per_batch):
    t, d = x.shape
    tb = PEER_TOKENS_PER_STEP
    ne = PEER_PER_TOKEN
    bpb = rows_per_batch // tb
    row = pl.BlockSpec((tb, 1, d), lambda i: (i, 0, 0))
    out = pl.pallas_call(
        _peer_kernel,
        grid=(t // tb,),
        in_specs=[pl.BlockSpec((tb, ne), lambda i: (i, 0), memory_space=pltpu.SMEM),
                  pl.BlockSpec((ne, tb), lambda i: (0, i)),
                  row, row,
                  pl.BlockSpec((1, 1, d), lambda i: (i // bpb, 0, 0)),
                  pl.BlockSpec(memory_space=pl.ANY)],
        out_specs=row,
        out_shape=jax.ShapeDtypeStruct((t, 1, d), F32),
        scratch_shapes=[pltpu.VMEM((PEER_SLOTS, ne, d), jnp.uint32),
                        pltpu.SemaphoreType.DMA((PEER_SLOTS,))],
        compiler_params=_params(("arbitrary",)),
        name="peer_experts",
    )(ids_t, gates, h.reshape(t, 1, d), x.reshape(t, 1, d), g2, table)
    return out.reshape(t, d)


def _rms_kernel(x_ref, g_ref, o_ref):
    x = x_ref[...]
    ms = jnp.mean(x * x, axis=-1, keepdims=True)
    o_ref[...] = (x * lax.rsqrt(ms + EPS)) * g_ref[...]


def _rms_norm(x, gamma, tm=512):
    t, d = x.shape
    row = pl.BlockSpec((tm, d), lambda i: (i, 0))
    return pl.pallas_call(
        _rms_kernel,
        grid=(t // tm,),
        in_specs=[row, pl.BlockSpec((1, d), lambda i: (0, 0))],
        out_specs=row,
        out_shape=jax.ShapeDtypeStruct((t, d), F32),
        compiler_params=_params(("arbitrary",)),
        name="final_rms_norm",
    )(x, gamma.reshape(1, d))


def _peer_layer(x, mod, gamma, w_q, sub_keys, u, v, seq):
    bsz, d = mod.shape[0], x.shape[1]
    sh2 = mod[:, 3 * d:4 * d].reshape(bsz, 1, d)
    sc2 = mod[:, 4 * d:5 * d].reshape(bsz, 1, d)
    g2 = mod[:, 5 * d:6 * d].reshape(bsz, 1, d)
    q, h = _norm_mod_matmul(x, gamma, sh2, sc2, w_q.astype(BF16), seq, 512, emit_h=True)
    ids, gates = _peer_route(q, sub_keys)
    table = _pack_experts(u, v)
    return _peer_experts(ids.T, gates, h, x, g2, table, seq)


def kernel(x, c, ctx, c_ctx, w_mod, b_mod, norm_mix_g, norm_ffn_g, norm_final_g, lru_w_in, lru_conv_w, lru_conv_b, lru_w_a, lru_b_a, lru_w_x, lru_b_x, lru_lambda, lru_w_out, sc_w_in, sc_conv_w, sc_conv_b, sc_w_out, peer_w_q, peer_sub_keys, peer_u, peer_v):
    bsz, seq, d = x.shape
    ctx_len = ctx.shape[1]
    xt = x.reshape(bsz * seq, d)

    c_rows = jnp.concatenate([c, c_ctx[None, :], jnp.zeros((SUBLANES - bsz - 1, d), F32)], axis=0)
    mods = [_mod_vectors(c_rows, w_mod[i], b_mod[i]) for i in range(w_mod.shape[0])]

    mod = mods[0][:bsz]
    sh1, sc1, g1 = (mod[:, k * d:(k + 1) * d].reshape(bsz, 1, d) for k in range(3))
    w_in16 = lru_w_in[0].astype(BF16)
    w_a16, w_x16 = lru_w_a[0].astype(BF16), lru_w_x[0].astype(BF16)
    z = _norm_mod_matmul(xt, norm_mix_g[0], sh1, sc1, w_in16, seq, 512)

    mod_c = mods[0][bsz:bsz + 1]
    sh_c = jnp.broadcast_to(mod_c[:, 0:d].reshape(1, 1, d), (bsz, 1, d))
    sc_c = jnp.broadcast_to(mod_c[:, d:2 * d].reshape(1, 1, d), (bsz, 1, d))
    z_ctx = _norm_mod_matmul(ctx.reshape(bsz * ctx_len, d), norm_mix_g[0], sh_c, sc_c, w_in16, ctx_len, ctx_len)
    coef_args = (1, lru_conv_w[0], lru_conv_b[0], w_a16, lru_b_a[0], w_x16, lru_b_x[0], lru_lambda[0])
    cf = [t.reshape(bsz, ctx_len, d) for t in _lru_coeffs(z_ctx, *coef_args, ctx_len, ctx_len)]
    zero_state = jnp.zeros((bsz, 1, d), F32)
    hc_f = _linear_scan(cf[0], cf[1], zero_state, False)
    hc_r = _linear_scan(cf[2], cf[3], zero_state, True)

    lf = [t.reshape(bsz, seq, d) for t in _lru_coeffs(z, *coef_args, GRID_W, 256)]
    h_f = _linear_scan(lf[0], lf[1], hc_f[:, ctx_len - 1:ctx_len, :], False)
    y = _linear_scan(lf[2], lf[3], hc_r[:, 0:1, :], True, prev=h_f)
    xt = _lru_out(z, y.reshape(bsz * seq, d), xt, g1, lru_w_out[0].astype(BF16), seq)
    xt = _peer_layer(xt, mod, norm_ffn_g[0], peer_w_q[0], peer_sub_keys[0], peer_u[0], peer_v[0], seq)

    mod = mods[1][:bsz]
    sh1, sc1, g1 = (mod[:, k * d:(k + 1) * d].reshape(bsz, 1, d) for k in range(3))
    z = _norm_mod_matmul(xt, norm_mix_g[1], sh1, sc1, sc_w_in[0].astype(BF16), seq, 512)
    xt = _sc_out(z, sc_conv_w[0], sc_conv_b[0], xt, g1, sc_w_out[0].astype(BF16), seq)
    xt = _peer_layer(xt, mod, norm_ffn_g[1], peer_w_q[1], peer_sub_keys[1], peer_u[1], peer_v[1], seq)

    return _rms_norm(xt, norm_final_g).reshape(bsz, seq, d)
```

```python
import functools

import jax
import jax.numpy as jnp
from jax import lax
from jax.experimental import pallas as pl
from jax.experimental.pallas import tpu as pltpu

EPS = 1e-6
GRID_W = 64
LRU_HEADS = 16
LRU_C = 8.0
PEER_HEADS = 8
PEER_N_KEYS = 128
PEER_TOPK = 16
PEER_PER_TOKEN = PEER_HEADS * PEER_TOPK

LANES = 128
SUBLANES = 8
VMEM_LIMIT = 48 * 1024 * 1024

F32 = jnp.float32
BF16 = jnp.bfloat16
HIGHEST = lax.Precision.HIGHEST


def _params(sem):
    return pltpu.CompilerParams(dimension_semantics=sem, vmem_limit_bytes=VMEM_LIMIT)


def _gelu_tanh(x):
    c = 0.7978845608028654
    return x * (0.5 * (1.0 + jnp.tanh(c * (x + 0.044715 * (x * x * x)))))


def _neg_expm1(z, exp_z):
    poly = z * (1.0 + z * (1 / 2 + z * (1 / 6 + z * (1 / 24 + z * (1 / 120 + z * (1 / 720 + z * (1 / 5040)))))))
    return jnp.where(z > -0.25, -poly, 1.0 - exp_z)


def _mod_kernel(c_ref, w_ref, b_ref, o_ref):
    c = c_ref[...]
    s = c * jax.nn.sigmoid(c)
    o_ref[...] = jnp.dot(s, w_ref[...], preferred_element_type=F32, precision=HIGHEST) + b_ref[...]


def _mod_vectors(c_rows, w, b):
    d, n = w.shape
    tn = 1024
    return pl.pallas_call(
        _mod_kernel,
        grid=(n // tn,),
        in_specs=[pl.BlockSpec((SUBLANES, d), lambda j: (0, 0)),
                  pl.BlockSpec((d, tn), lambda j: (0, j)),
                  pl.BlockSpec((1, tn), lambda j: (0, j))],
        out_specs=pl.BlockSpec((SUBLANES, tn), lambda j: (0, j)),
        out_shape=jax.ShapeDtypeStruct((SUBLANES, n), F32),
        compiler_params=_params(("arbitrary",)),
        name="mod_vectors",
    )(c_rows, w, b.reshape(1, n))


def _nmm_kernel(emit_h, x_ref, g_ref, sh_ref, sc_ref, w_ref, o_ref, *rest):
    if emit_h:
        h_ref, hn_ref = rest
    else:
        (hn_ref,) = rest

    @pl.when(pl.program_id(1) == 0)
    def _():
        x = x_ref[...]
        ms = jnp.mean(x * x, axis=-1, keepdims=True)
        y = (x * lax.rsqrt(ms + EPS)) * g_ref[...]
        h = y * (1.0 + sc_ref[0]) + sh_ref[0]
        hn_ref[...] = h.astype(BF16)
        if emit_h:
            h_ref[...] = h

    o_ref[...] = jnp.dot(hn_ref[...], w_ref[...], preferred_element_type=F32)


def _norm_mod_matmul(x, gamma, shift, scale, w_bf16, rows_per_batch, tm, emit_h=False):
    t, d = x.shape
    n = w_bf16.shape[1]
    tn = 1024
    bpb = rows_per_batch // tm
    out_shape = [jax.ShapeDtypeStruct((t, n), F32)]
    out_specs = [pl.BlockSpec((tm, tn), lambda i, j: (i, j))]
    if emit_h:
        out_shape.append(jax.ShapeDtypeStruct((t, d), F32))
        out_specs.append(pl.BlockSpec((tm, d), lambda i, j: (i, 0)))
    res = pl.pallas_call(
        functools.partial(_nmm_kernel, emit_h),
        grid=(t // tm, n // tn),
        in_specs=[pl.BlockSpec((tm, d), lambda i, j: (i, 0)),
                  pl.BlockSpec((1, d), lambda i, j: (0, 0)),
                  pl.BlockSpec((1, 1, d), lambda i, j: (i // bpb, 0, 0)),
                  pl.BlockSpec((1, 1, d), lambda i, j: (i // bpb, 0, 0)),
                  pl.BlockSpec((d, tn), lambda i, j: (0, j))],
        out_specs=out_specs,
        out_shape=out_shape,
        scratch_shapes=[pltpu.VMEM((tm, d), BF16)],
        compiler_params=_params(("arbitrary", "arbitrary")),
        name="norm_mod_matmul",
    )(x, gamma.reshape(1, d), shift, scale, w_bf16)
    return res if emit_h else res[0]


def _row_conv(u, taps, bias, pad_left, row_len):
    tm = u.shape[0]
    pos = lax.broadcasted_iota(jnp.int32, (tm, 1), 0) % row_len
    y = None
    for k in range(len(taps)):
        off = k - pad_left
        if off == 0:
            term = u
        else:
            rolled = pltpu.roll(u, (-off) % tm, axis=0)
            valid = jnp.logical_and(pos + off >= 0, pos + off < row_len)
            term = jnp.where(valid, rolled, 0.0)
        y = (bias + taps[k] * term) if y is None else (y + taps[k] * term)
    return y


def _lru_coef_kernel(row_len, xb_ref, cw_ref, cb_ref, wa_ref, ba_ref, wx_ref, bx_ref, lam_ref,
                     af_ref, bf_ref, ar_ref, br_ref):
    taps = [cw_ref[k:k + 1, :] for k in range(cw_ref.shape[0])]
    xc = _row_conv(xb_ref[...], taps, cb_ref[...], 2, row_len)
    d = xc.shape[1]
    hd = d // LRU_HEADS
    outs = ((af_ref, bf_ref), (ar_ref, br_ref))
    for dr in range(2):
        lam = lam_ref[dr:dr + 1, :]
        z = -lam
        sp = jnp.maximum(z, 0.0) + jnp.log1p(jnp.exp(-jnp.abs(z)))
        a_ref, b_ref = outs[dr]
        for h in range(LRU_HEADS):
            sl = slice(h * hd, (h + 1) * hd)
            xh = xc[:, sl]
            xh16 = xh.astype(BF16)
            r = jax.nn.sigmoid(jnp.dot(xh16, wa_ref[dr, h], preferred_element_type=F32) + ba_ref[dr:dr + 1, sl])
            i = jax.nn.sigmoid(jnp.dot(xh16, wx_ref[dr, h], preferred_element_type=F32) + bx_ref[dr:dr + 1, sl])
            log_a = (-LRU_C) * r * sp[:, sl]
            a = jnp.exp(log_a)
            a_ref[:, sl] = a
            b_ref[:, sl] = jnp.sqrt(_neg_expm1(2.0 * log_a, a * a)) * (i * xh)


def _lru_coeffs(z, col_block, conv_w, conv_b, w_a16, b_a, w_x16, b_x, lam, row_len, tm):
    t = z.shape[0]
    d = conv_w.shape[1]
    full = lambda *s: pl.BlockSpec(s, lambda i: (0,) * len(s))
    o_spec = pl.BlockSpec((tm, d), lambda i: (i, 0))
    o_shape = jax.ShapeDtypeStruct((t, d), F32)
    return pl.pallas_call(
        functools.partial(_lru_coef_kernel, row_len),
        grid=(t // tm,),
        in_specs=[pl.BlockSpec((tm, d), lambda i: (i, col_block)),
                  full(*conv_w.shape), full(1, d),
                  full(*w_a16.shape), full(2, d), full(*w_x16.shape), full(2, d), full(2, d)],
        out_specs=[o_spec] * 4,
        out_shape=[o_shape] * 4,
        compiler_params=_params(("arbitrary",)),
        name="lru_coeffs",
    )(z, conv_w, conv_b.reshape(1, d), w_a16, b_a, w_x16, b_x, lam)


def _scan_kernel(reverse, add_prev, a_ref, b_ref, h0_ref, *rest):
    if add_prev:
        y_ref, o_ref, st_ref = rest
    else:
        o_ref, st_ref = rest
    ts, cw = a_ref.shape
    ng = ts // SUBLANES

    @pl.when(pl.program_id(2) == 0)
    def _():
        st_ref[...] = jnp.broadcast_to(h0_ref[...], st_ref.shape)

    row = lax.broadcasted_iota(jnp.int32, (SUBLANES, cw), 0)

    def body(gi, h):
        g = (ng - 1 - gi) if reverse else gi
        off = pl.multiple_of(g * SUBLANES, SUBLANES)
        a = a_ref[pl.ds(off, SUBLANES), :]
        b = b_ref[pl.ds(off, SUBLANES), :]
        for s in (1, 2, 4):
            if reverse:
                shift, m = SUBLANES - s, row < SUBLANES - s
            else:
                shift, m = s, row >= s
            a_s = jnp.where(m, pltpu.roll(a, shift, axis=0), 1.0)
            b_s = jnp.where(m, pltpu.roll(b, shift, axis=0), 0.0)
            b = a * b_s + b
            a = a * a_s
        hh = a * h + b
        if add_prev:
            o_ref[pl.ds(off, SUBLANES), :] = hh + y_ref[pl.ds(off, SUBLANES), :]
        else:
            o_ref[pl.ds(off, SUBLANES), :] = hh
        last = hh[0:1, :] if reverse else hh[SUBLANES - 1:SUBLANES, :]
        return jnp.broadcast_to(last, (SUBLANES, cw))

    st_ref[...] = lax.fori_loop(0, ng, body, st_ref[...], unroll=2)


def _linear_scan(a, b, h0, reverse, prev=None, ts=512, cw=512):
    bsz, s, c = a.shape
    ts = min(ts, s)
    ns = s // ts
    tmap = (lambda bb, cc, j: (bb, ns - 1 - j, cc)) if reverse else (lambda bb, cc, j: (bb, j, cc))
    blk = pl.BlockSpec((None, ts, cw), tmap)
    ins = [a, b, h0]
    in_specs = [blk, blk, pl.BlockSpec((None, 1, cw), lambda bb, cc, j: (bb, 0, cc))]
    if prev is not None:
        ins.append(prev)
        in_specs.append(blk)
    return pl.pallas_call(
        functools.partial(_scan_kernel, reverse, prev is not None),
        grid=(bsz, c // cw, ns),
        in_specs=in_specs,
        out_specs=blk,
        out_shape=jax.ShapeDtypeStruct((bsz, s, c), F32),
        scratch_shapes=[pltpu.VMEM((SUBLANES, cw), F32)],
        compiler_params=_params(("arbitrary", "arbitrary", "arbitrary")),
        name="linear_scan",
    )(*ins)


def _lru_out_kernel(gate_ref, y_ref, x_ref, g1_ref, w_ref, o_ref):
    p = _gelu_tanh(gate_ref[...]) * y_ref[...]
    out = jnp.dot(p.astype(BF16), w_ref[...], preferred_element_type=F32)
    o_ref[...] = x_ref[...] + g1_ref[0] * out


def _lru_out(z, y, x, g1, w16, rows_per_batch, tm=256):
    t, d = x.shape
    bpb = rows_per_batch // tm
    row = pl.BlockSpec((tm, d), lambda i: (i, 0))
    return pl.pallas_call(
        _lru_out_kernel,
        grid=(t // tm,),
        in_specs=[row, row, row,
                  pl.BlockSpec((1, 1, d), lambda i: (i // bpb, 0, 0)),
                  pl.BlockSpec((d, d), lambda i: (0, 0))],
        out_specs=row,
        out_shape=jax.ShapeDtypeStruct((t, d), F32),
        compiler_params=_params(("arbitrary",)),
        name="lru_out",
    )(z, y, x, g1, w16)


def _sc_out_kernel(bg_ref, cg_ref, v_ref, cw_ref, cb_ref, x_ref, g1_ref, w_ref, o_ref):
    taps = [cw_ref[k:k + 1, :] for k in range(cw_ref.shape[0])]
    y = _row_conv(cg_ref[...] * v_ref[...], taps, cb_ref[...], 1, GRID_W)
    p = bg_ref[...] * y
    out = jnp.dot(p.astype(BF16), w_ref[...], preferred_element_type=F32)
    o_ref[...] = x_ref[...] + g1_ref[0] * out


def _sc_out(z, conv_w, conv_b, x, g1, w16, rows_per_batch, tm=256):
    t, d = x.shape
    bpb = rows_per_batch // tm
    row = pl.BlockSpec((tm, d), lambda i: (i, 0))
    col = lambda c: pl.BlockSpec((tm, d), lambda i: (i, c))
    return pl.pallas_call(
        _sc_out_kernel,
        grid=(t // tm,),
        in_specs=[col(0), col(1), col(2),
                  pl.BlockSpec(conv_w.shape, lambda i: (0, 0)),
                  pl.BlockSpec((1, d), lambda i: (0, 0)),
                  row,
                  pl.BlockSpec((1, 1, d), lambda i: (i // bpb, 0, 0)),
                  pl.BlockSpec((d, d), lambda i: (0, 0))],
        out_specs=row,
        out_shape=jax.ShapeDtypeStruct((t, d), F32),
        compiler_params=_params(("arbitrary",)),
        name="sc_out",
    )(z, z, z, conv_w, conv_b.reshape(1, d), x, g1, w16)


def _top_rows(vals, idx_f, fill, n_out, emit):
    for r in range(n_out):
        m = jnp.max(vals, axis=0, keepdims=True)
        am = jnp.min(jnp.where(vals == m, idx_f, fill), axis=0, keepdims=True)
        emit(r, m, am)
        vals = jnp.where(idx_f == am, -jnp.inf, vals)


def _route_kernel(q_ref, k_ref, ids_ref, gate_ref, sv_ref, si_ref, cv_ref):
    tb = q_ref.shape[0]
    nk, dk = k_ref.shape[1], k_ref.shape[2]
    kk = PEER_TOPK
    key_idx = lax.broadcasted_iota(jnp.int32, (nk, tb), 0).astype(F32)
    for p in range(2):
        s_t = lax.dot_general(k_ref[p], q_ref[:, p * dk:(p + 1) * dk], (((1,), (1,)), ((), ())),
                              precision=HIGHEST, preferred_element_type=F32)

        def emit(r, m, am, p=p):
            sv_ref[p, r:r + 1, :] = m
            si_ref[p, r:r + 1, :] = am

        _top_rows(s_t, key_idx, float(nk), kk, emit)

    sv0, sv1 = sv_ref[0], sv_ref[1]
    si0, si1 = si_ref[0], si_ref[1]
    cand = jnp.concatenate([sv0[i:i + 1, :] + sv1 for i in range(kk)], axis=0)
    eid = jnp.concatenate([si0[i:i + 1, :] * float(nk) + si1 for i in range(kk)], axis=0)
    cand_idx = lax.broadcasted_iota(jnp.int32, (kk * kk, tb), 0).astype(F32)

    def emit2(r, m, am):
        e = jnp.max(jnp.where(cand_idx == am, eid, -1.0), axis=0, keepdims=True)
        cv_ref[r:r + 1, :] = m
        ids_ref[r:r + 1, :] = e.astype(jnp.int32)

    _top_rows(cand, cand_idx, float(kk * kk), kk, emit2)
    cv = cv_ref[...]
    ex = jnp.exp(cv - jnp.max(cv, axis=0, keepdims=True))
    gate_ref[...] = ex / jnp.sum(ex, axis=0, keepdims=True)


def _peer_route(q, sub_keys, tb=256):
    t = q.shape[0]
    nh, _, nk, dk = sub_keys.shape
    kk = PEER_TOPK
    o_spec = pl.BlockSpec((kk, tb), lambda i, h: (h, i))
    return pl.pallas_call(
        _route_kernel,
        grid=(t // tb, nh),
        in_specs=[pl.BlockSpec((tb, 2 * dk), lambda i, h: (i, h)),
                  pl.BlockSpec((None, 2, nk, dk), lambda i, h: (h, 0, 0, 0))],
        out_specs=[o_spec, o_spec],
        out_shape=[jax.ShapeDtypeStruct((nh * kk, t), jnp.int32),
                   jax.ShapeDtypeStruct((nh * kk, t), F32)],
        scratch_shapes=[pltpu.VMEM((2, kk, tb), F32), pltpu.VMEM((2, kk, tb), F32), pltpu.VMEM((kk, tb), F32)],
        compiler_params=_params(("arbitrary", "arbitrary")),
        name="peer_route",
    )(q, sub_keys)


def _pack_kernel(u_ref, v_ref, o_ref):
    ub = pltpu.bitcast(u_ref[...].astype(BF16).astype(F32), jnp.uint32)
    vb = pltpu.bitcast(v_ref[...].astype(BF16).astype(F32), jnp.uint32)
    o_ref[...] = ub | (vb >> 16)


def _pack_experts(u, v, te=512):
    e, d = u.shape
    blk = pl.BlockSpec((te, d), lambda i: (i, 0))
    return pl.pallas_call(
        _pack_kernel,
        grid=(e // te,),
        in_specs=[blk, blk],
        out_specs=blk,
        out_shape=jax.ShapeDtypeStruct((e, d), jnp.uint32),
        compiler_params=_params(("arbitrary",)),
        name="pack_experts",
    )(u, v)


PEER_TOKENS_PER_STEP = 128
PEER_SLOTS = 4


def _peer_kernel(ids_ref, gate_ref, h_ref, x_ref, g2_ref, tab_ref, o_ref, buf_ref, sem_ref):
    tb, _, d = h_ref.shape
    ne = PEER_PER_TOKEN
    nslots = buf_ref.shape[0]
    nchunk = d // LANES
    ahead = nslots - 1

    def start_rows(tok, slot, k0, k1):
        rows = [ids_ref[tok, k] for k in range(k0, k1)]
        for k, row in zip(range(k0, k1), rows):
            pltpu.make_async_copy(tab_ref.at[row], buf_ref.at[slot, pl.ds(k, 1), :],
                                  sem_ref.at[slot]).start(priority=k % 2)

    def wait_token(slot):
        pltpu.make_async_copy(tab_ref.at[pl.ds(0, ne), 0], buf_ref.at[slot], sem_ref.at[slot]).wait()

    lane_tok = lax.broadcasted_iota(jnp.int32, (ne, tb), 1)

    def token(tok, slot, prefetch):
        nxt_slot = (slot + ahead) % nslots

        def issue(step):
            if prefetch and step == 2 * nchunk - 1:
                start_rows(tok + ahead, nxt_slot, 0, ne)

        wait_token(slot)
        xrow = h_ref[tok]
        acc = jnp.zeros((ne, LANES), F32)
        for c in range(nchunk):
            w32 = buf_ref[slot, :, c * LANES:(c + 1) * LANES]
            u = pltpu.bitcast(w32 & jnp.uint32(0xFFFF0000), F32)
            acc = acc + u * xrow[:, c * LANES:(c + 1) * LANES]
            issue(c)
        act = _gelu_tanh(jnp.sum(acc, axis=1, keepdims=True))
        gcol = jnp.sum(jnp.where(lane_tok == tok, gate_ref[...], 0.0), axis=1, keepdims=True)
        wb = jnp.broadcast_to(gcol * act, (ne, LANES))
        for c in range(nchunk):
            w32 = buf_ref[slot, :, c * LANES:(c + 1) * LANES]
            v = pltpu.bitcast(w32 << 16, F32)
            oc = jnp.sum(v * wb, axis=0, keepdims=True)
            sl = slice(c * LANES, (c + 1) * LANES)
            o_ref[tok, :, sl] = x_ref[tok, :, sl] + g2_ref[0][:, sl] * oc
            issue(nchunk + c)

    for s in range(ahead):
        start_rows(s, s, 0, ne)

    def group(g, carry):
        for j in range(nslots):
            token(g * nslots + j, j, True)
        return carry

    ngroups = tb // nslots
    lax.fori_loop(0, ngroups - 1, group, 0)
    for j in range(nslots):
        tok = (ngroups - 1) * nslots + j
        token(tok, j, tok + ahead < tb)


def _peer_experts(ids_t, gates, h, x, g2, table, rows_per_batch):
    t, d = x.shape
    tb = PEER_TOKENS_PER_STEP
    ne = PEER_PER_TOKEN
    bpb = rows_per_batch // tb
    row = pl.BlockSpec((tb, 1, d), lambda i: (i, 0, 0))
    out = pl.pallas_call(
        _peer_kernel,
        grid=(t // tb,),
        in_specs=[pl.BlockSpec((tb, ne), lambda i: (i, 0), memory_space=pltpu.SMEM),
                  pl.BlockSpec((ne, tb), lambda i: (0, i)),
                  row, row,
                  pl.BlockSpec((1, 1, d), lambda i: (i // bpb, 0, 0)),
                  pl.BlockSpec(memory_space=pl.ANY)],
        out_specs=row,
        out_shape=jax.ShapeDtypeStruct((t, 1, d), F32),
        scratch_shapes=[pltpu.VMEM((PEER_SLOTS, ne, d), jnp.uint32),
                        pltpu.SemaphoreType.DMA((PEER_SLOTS,))],
        compiler_params=_params(("arbitrary",)),
        name="peer_experts",
    )(ids_t, gates, h.reshape(t, 1, d), x.reshape(t, 1, d), g2, table.reshape(table.shape[0], 1, d))
    return out.reshape(t, d)


def _rms_kernel(x_ref, g_ref, o_ref):
    x = x_ref[...]
    ms = jnp.mean(x * x, axis=-1, keepdims=True)
    o_ref[...] = (x * lax.rsqrt(ms + EPS)) * g_ref[...]


def _rms_norm(x, gamma, tm=512):
    t, d = x.shape
    row = pl.BlockSpec((tm, d), lambda i: (i, 0))
    return pl.pallas_call(
        _rms_kernel,
        grid=(t // tm,),
        in_specs=[row, pl.BlockSpec((1, d), lambda i: (0, 0))],
        out_specs=row,
        out_shape=jax.ShapeDtypeStruct((t, d), F32),
        compiler_params=_params(("arbitrary",)),
        name="final_rms_norm",
    )(x, gamma.reshape(1, d))


def _peer_layer(x, mod, gamma, w_q, sub_keys, u, v, seq):
    bsz, d = mod.shape[0], x.shape[1]
    sh2 = mod[:, 3 * d:4 * d].reshape(bsz, 1, d)
    sc2 = mod[:, 4 * d:5 * d].reshape(bsz, 1, d)
    g2 = mod[:, 5 * d:6 * d].reshape(bsz, 1, d)
    q, h = _norm_mod_matmul(x, gamma, sh2, sc2, w_q.astype(BF16), seq, 512, emit_h=True)
    ids, gates = _peer_route(q, sub_keys)
    table = _pack_experts(u, v)
    return _peer_experts(ids.T, gates, h, x, g2, table, seq)


def kernel(x, c, ctx, c_ctx, w_mod, b_mod, norm_mix_g, norm_ffn_g, norm_final_g, lru_w_in, lru_conv_w, lru_conv_b, lru_w_a, lru_b_a, lru_w_x, lru_b_x, lru_lambda, lru_w_out, sc_w_in, sc_conv_w, sc_conv_b, sc_w_out, peer_w_q, peer_sub_keys, peer_u, peer_v):
    bsz, seq, d = x.shape
    ctx_len = ctx.shape[1]
    xt = x.reshape(bsz * seq, d)

    c_rows = jnp.concatenate([c, c_ctx[None, :], jnp.zeros((SUBLANES - bsz - 1, d), F32)], axis=0)
    mods = [_mod_vectors(c_rows, w_mod[i], b_mod[i]) for i in range(w_mod.shape[0])]

    mod = mods[0][:bsz]
    sh1, sc1, g1 = (mod[:, k * d:(k + 1) * d].reshape(bsz, 1, d) for k in range(3))
    w_in16 = lru_w_in[0].astype(BF16)
    w_a16, w_x16 = lru_w_a[0].astype(BF16), lru_w_x[0].astype(BF16)
    z = _norm_mod_matmul(xt, norm_mix_g[0], sh1, sc1, w_in16, seq, 512)

    mod_c = mods[0][bsz:bsz + 1]
    sh_c = jnp.broadcast_to(mod_c[:, 0:d].reshape(1, 1, d), (bsz, 1, d))
    sc_c = jnp.broadcast_to(mod_c[:, d:2 * d].reshape(1, 1, d), (bsz, 1, d))
    z_ctx = _norm_mod_matmul(ctx.reshape(bsz * ctx_len, d), norm_mix_g[0], sh_c, sc_c, w_in16, ctx_len, ctx_len)
    coef_args = (1, lru_conv_w[0], lru_conv_b[0], w_a16, lru_b_a[0], w_x16, lru_b_x[0], lru_lambda[0])
    cf = [t.reshape(bsz, ctx_len, d) for t in _lru_coeffs(z_ctx, *coef_args, ctx_len, ctx_len)]
    zero_state = jnp.zeros((bsz, 1, d), F32)
    hc_f = _linear_scan(cf[0], cf[1], zero_state, False)
    hc_r = _linear_scan(cf[2], cf[3], zero_state, True)

    lf = [t.reshape(bsz, seq, d) for t in _lru_coeffs(z, *coef_args, GRID_W, 256)]
    h_f = _linear_scan(lf[0], lf[1], hc_f[:, ctx_len - 1:ctx_len, :], False)
    y = _linear_scan(lf[2], lf[3], hc_r[:, 0:1, :], True, prev=h_f)
    xt = _lru_out(z, y.reshape(bsz * seq, d), xt, g1, lru_w_out[0].astype(BF16), seq)
    xt = _peer_layer(xt, mod, norm_ffn_g[0], peer_w_q[0], peer_sub_keys[0], peer_u[0], peer_v[0], seq)

    mod = mods[1][:bsz]
    sh1, sc1, g1 = (mod[:, k * d:(k + 1) * d].reshape(bsz, 1, d) for k in range(3))
    z = _norm_mod_matmul(xt, norm_mix_g[1], sh1, sc1, sc_w_in[0].astype(BF16), seq, 512)
    xt = _sc_out(z, sc_conv_w[0], sc_conv_b[0], xt, g1, sc_w_out[0].astype(BF16), seq)
    xt = _peer_layer(xt, mod, norm_ffn_g[1], peer_w_q[1], peer_sub_keys[1], peer_u[1], peer_v[1], seq)

    return _rms_norm(xt, norm_final_g).reshape(bsz, seq, d)
```

```python
import functools

import jax
import jax.numpy as jnp
from jax import lax
from jax.experimental import pallas as pl
from jax.experimental.pallas import tpu as pltpu

EPS = 1e-6
GRID_W = 64
LRU_HEADS = 16
LRU_C = 8.0
PEER_HEADS = 8
PEER_N_KEYS = 128
PEER_TOPK = 16
PEER_PER_TOKEN = PEER_HEADS * PEER_TOPK

LANES = 128
SUBLANES = 8
VMEM_LIMIT = 48 * 1024 * 1024

F32 = jnp.float32
BF16 = jnp.bfloat16
HIGHEST = lax.Precision.HIGHEST


def _params(sem):
    return pltpu.CompilerParams(dimension_semantics=sem, vmem_limit_bytes=VMEM_LIMIT)


def _gelu_tanh(x):
    c = 0.7978845608028654
    return x * (0.5 * (1.0 + jnp.tanh(c * (x + 0.044715 * (x * x * x)))))


def _neg_expm1(z, exp_z):
    poly = z * (1.0 + z * (1 / 2 + z * (1 / 6 + z * (1 / 24 + z * (1 / 120 + z * (1 / 720 + z * (1 / 5040)))))))
    return jnp.where(z > -0.25, -poly, 1.0 - exp_z)


def _mod_kernel(c_ref, w_ref, b_ref, o_ref):
    c = c_ref[...]
    s = c * jax.nn.sigmoid(c)
    o_ref[...] = jnp.dot(s, w_ref[...], preferred_element_type=F32, precision=HIGHEST) + b_ref[...]


def _mod_vectors(c_rows, w, b):
    d, n = w.shape
    tn = 1024
    return pl.pallas_call(
        _mod_kernel,
        grid=(n // tn,),
        in_specs=[pl.BlockSpec((SUBLANES, d), lambda j: (0, 0)),
                  pl.BlockSpec((d, tn), lambda j: (0, j)),
                  pl.BlockSpec((1, tn), lambda j: (0, j))],
        out_specs=pl.BlockSpec((SUBLANES, tn), lambda j: (0, j)),
        out_shape=jax.ShapeDtypeStruct((SUBLANES, n), F32),
        compiler_params=_params(("arbitrary",)),
        name="mod_vectors",
    )(c_rows, w, b.reshape(1, n))


def _nmm_kernel(emit_h, x_ref, g_ref, sh_ref, sc_ref, w_ref, o_ref, *rest):
    if emit_h:
        h_ref, hn_ref = rest
    else:
        (hn_ref,) = rest

    @pl.when(pl.program_id(1) == 0)
    def _():
        x = x_ref[...]
        ms = jnp.mean(x * x, axis=-1, keepdims=True)
        y = (x * lax.rsqrt(ms + EPS)) * g_ref[...]
        h = y * (1.0 + sc_ref[0]) + sh_ref[0]
        hn_ref[...] = h.astype(BF16)
        if emit_h:
            h_ref[...] = h

    o_ref[...] = jnp.dot(hn_ref[...], w_ref[...], preferred_element_type=F32)


def _norm_mod_matmul(x, gamma, shift, scale, w_bf16, rows_per_batch, tm, emit_h=False):
    t, d = x.shape
    n = w_bf16.shape[1]
    tn = 1024
    bpb = rows_per_batch // tm
    out_shape = [jax.ShapeDtypeStruct((t, n), F32)]
    out_specs = [pl.BlockSpec((tm, tn), lambda i, j: (i, j))]
    if emit_h:
        out_shape.append(jax.ShapeDtypeStruct((t, d), F32))
        out_specs.append(pl.BlockSpec((tm, d), lambda i, j: (i, 0)))
    res = pl.pallas_call(
        functools.partial(_nmm_kernel, emit_h),
        grid=(t // tm, n // tn),
        in_specs=[pl.BlockSpec((tm, d), lambda i, j: (i, 0)),
                  pl.BlockSpec((1, d), lambda i, j: (0, 0)),
                  pl.BlockSpec((1, 1, d), lambda i, j: (i // bpb, 0, 0)),
                  pl.BlockSpec((1, 1, d), lambda i, j: (i // bpb, 0, 0)),
                  pl.BlockSpec((d, tn), lambda i, j: (0, j))],
        out_specs=out_specs,
        out_shape=out_shape,
        scratch_shapes=[pltpu.VMEM((tm, d), BF16)],
        compiler_params=_params(("arbitrary", "arbitrary")),
        name="norm_mod_matmul",
    )(x, gamma.reshape(1, d), shift, scale, w_bf16)
    return res if emit_h else res[0]


def _row_conv(u, taps, bias, pad_left, row_len):
    tm = u.shape[0]
    pos = lax.broadcasted_iota(jnp.int32, (tm, 1), 0) % row_len
    y = None
    for k in range(len(taps)):
        off = k - pad_left
        if off == 0:
            term = u
        else:
            rolled = pltpu.roll(u, (-off) % tm, axis=0)
            valid = jnp.logical_and(pos + off >= 0, pos + off < row_len)
            term = jnp.where(valid, rolled, 0.0)
        y = (bias + taps[k] * term) if y is None else (y + taps[k] * term)
    return y


def _lru_coef_kernel(row_len, xb_ref, cw_ref, cb_ref, wa_ref, ba_ref, wx_ref, bx_ref, lam_ref,
                     af_ref, bf_ref, ar_ref, br_ref):
    taps = [cw_ref[k:k + 1, :] for k in range(cw_ref.shape[0])]
    xc = _row_conv(xb_ref[...], taps, cb_ref[...], 2, row_len)
    d = xc.shape[1]
    hd = d // LRU_HEADS
    outs = ((af_ref, bf_ref), (ar_ref, br_ref))
    for dr in range(2):
        lam = lam_ref[dr:dr + 1, :]
        z = -lam
        sp = jnp.maximum(z, 0.0) + jnp.log1p(jnp.exp(-jnp.abs(z)))
        a_ref, b_ref = outs[dr]
        for h in range(LRU_HEADS):
            sl = slice(h * hd, (h + 1) * hd)
            xh = xc[:, sl]
            xh16 = xh.astype(BF16)
            r = jax.nn.sigmoid(jnp.dot(xh16, wa_ref[dr, h], preferred_element_type=F32) + ba_ref[dr:dr + 1, sl])
            i = jax.nn.sigmoid(jnp.dot(xh16, wx_ref[dr, h], preferred_element_type=F32) + bx_ref[dr:dr + 1, sl])
            log_a = (-LRU_C) * r * sp[:, sl]
            a = jnp.exp(log_a)
            a_ref[:, sl] = a
            b_ref[:, sl] = jnp.sqrt(_neg_expm1(2.0 * log_a, a * a)) * (i * xh)


def _lru_coeffs(z, col_block, conv_w, conv_b, w_a16, b_a, w_x16, b_x, lam, row_len, tm):
    t = z.shape[0]
    d = conv_w.shape[1]
    full = lambda *s: pl.BlockSpec(s, lambda i: (0,) * len(s))
    o_spec = pl.BlockSpec((tm, d), lambda i: (i, 0))
    o_shape = jax.ShapeDtypeStruct((t, d), F32)
    return pl.pallas_call(
        functools.partial(_lru_coef_kernel, row_len),
        grid=(t // tm,),
        in_specs=[pl.BlockSpec((tm, d), lambda i: (i, col_block)),
                  full(*conv_w.shape), full(1, d),
                  full(*w_a16.shape), full(2, d), full(*w_x16.shape), full(2, d), full(2, d)],
        out_specs=[o_spec] * 4,
        out_shape=[o_shape] * 4,
        compiler_params=_params(("arbitrary",)),
        name="lru_coeffs",
    )(z, conv_w, conv_b.reshape(1, d), w_a16, b_a, w_x16, b_x, lam)


def _scan_kernel(reverse, add_prev, a_ref, b_ref, h0_ref, *rest):
    if add_prev:
        y_ref, o_ref, st_ref = rest
    else:
        o_ref, st_ref = rest
    ts, cw = a_ref.shape
    ng = ts // SUBLANES

    @pl.when(pl.program_id(2) == 0)
    def _():
        st_ref[...] = jnp.broadcast_to(h0_ref[...], st_ref.shape)

    row = lax.broadcasted_iota(jnp.int32, (SUBLANES, cw), 0)

    def body(gi, h):
        g = (ng - 1 - gi) if reverse else gi
        off = pl.multiple_of(g * SUBLANES, SUBLANES)
        a = a_ref[pl.ds(off, SUBLANES), :]
        b = b_ref[pl.ds(off, SUBLANES), :]
        for s in (1, 2, 4):
            if reverse:
                shift, m = SUBLANES - s, row < SUBLANES - s
            else:
                shift, m = s, row >= s
            a_s = jnp.where(m, pltpu.roll(a, shift, axis=0), 1.0)
            b_s = jnp.where(m, pltpu.roll(b, shift, axis=0), 0.0)
            b = a * b_s + b
            a = a * a_s
        hh = a * h + b
        if add_prev:
            o_ref[pl.ds(off, SUBLANES), :] = hh + y_ref[pl.ds(off, SUBLANES), :]
        else:
            o_ref[pl.ds(off, SUBLANES), :] = hh
        last = hh[0:1, :] if reverse else hh[SUBLANES - 1:SUBLANES, :]
        return jnp.broadcast_to(last, (SUBLANES, cw))

    st_ref[...] = lax.fori_loop(0, ng, body, st_ref[...], unroll=2)


def _linear_scan(a, b, h0, reverse, prev=None, ts=512, cw=512):
    bsz, s, c = a.shape
    ts = min(ts, s)
    ns = s // ts
    tmap = (lambda bb, cc, j: (bb, ns - 1 - j, cc)) if reverse else (lambda bb, cc, j: (bb, j, cc))
    blk = pl.BlockSpec((None, ts, cw), tmap)
    ins = [a, b, h0]
    in_specs = [blk, blk, pl.BlockSpec((None, 1, cw), lambda bb, cc, j: (bb, 0, cc))]
    if prev is not None:
        ins.append(prev)
        in_specs.append(blk)
    return pl.pallas_call(
        functools.partial(_scan_kernel, reverse, prev is not None),
        grid=(bsz, c // cw, ns),
        in_specs=in_specs,
        out_specs=blk,
        out_shape=jax.ShapeDtypeStruct((bsz, s, c), F32),
        scratch_shapes=[pltpu.VMEM((SUBLANES, cw), F32)],
        compiler_params=_params(("arbitrary", "arbitrary", "arbitrary")),
        name="linear_scan",
    )(*ins)


def _lru_out_kernel(gate_ref, y_ref, x_ref, g1_ref, w_ref, o_ref):
    p = _gelu_tanh(gate_ref[...]) * y_ref[...]
    out = jnp.dot(p.astype(BF16), w_ref[...], preferred_element_type=F32)
    o_ref[...] = x_ref[...] + g1_ref[0] * out


def _lru_out(z, y, x, g1, w16, rows_per_batch, tm=256):
    t, d = x.shape
    bpb = rows_per_batch // tm
    row = pl.BlockSpec((tm, d), lambda i: (i, 0))
    return pl.pallas_call(
        _lru_out_kernel,
        grid=(t // tm,),
        in_specs=[row, row, row,
                  pl.BlockSpec((1, 1, d), lambda i: (i // bpb, 0, 0)),
                  pl.BlockSpec((d, d), lambda i: (0, 0))],
        out_specs=row,
        out_shape=jax.ShapeDtypeStruct((t, d), F32),
        compiler_params=_params(("arbitrary",)),
        name="lru_out",
    )(z, y, x, g1, w16)


def _sc_out_kernel(bg_ref, cg_ref, v_ref, cw_ref, cb_ref, x_ref, g1_ref, w_ref, o_ref):
    taps = [cw_ref[k:k + 1, :] for k in range(cw_ref.shape[0])]
    y = _row_conv(cg_ref[...] * v_ref[...], taps, cb_ref[...], 1, GRID_W)
    p = bg_ref[...] * y
    out = jnp.dot(p.astype(BF16), w_ref[...], preferred_element_type=F32)
    o_ref[...] = x_ref[...] + g1_ref[0] * out


def _sc_out(z, conv_w, conv_b, x, g1, w16, rows_per_batch, tm=256):
    t, d = x.shape
    bpb = rows_per_batch // tm
    row = pl.BlockSpec((tm, d), lambda i: (i, 0))
    col = lambda c: pl.BlockSpec((tm, d), lambda i: (i, c))
    return pl.pallas_call(
        _sc_out_kernel,
        grid=(t // tm,),
        in_specs=[col(0), col(1), col(2),
                  pl.BlockSpec(conv_w.shape, lambda i: (0, 0)),
                  pl.BlockSpec((1, d), lambda i: (0, 0)),
                  row,
                  pl.BlockSpec((1, 1, d), lambda i: (i // bpb, 0, 0)),
                  pl.BlockSpec((d, d), lambda i: (0, 0))],
        out_specs=row,
        out_shape=jax.ShapeDtypeStruct((t, d), F32),
        compiler_params=_params(("arbitrary",)),
        name="sc_out",
    )(z, z, z, conv_w, conv_b.reshape(1, d), x, g1, w16)


def _top_rows(vals, idx_f, fill, n_out, emit):
    for r in range(n_out):
        m = jnp.max(vals, axis=0, keepdims=True)
        am = jnp.min(jnp.where(vals == m, idx_f, fill), axis=0, keepdims=True)
        emit(r, m, am)
        vals = jnp.where(idx_f == am, -jnp.inf, vals)


def _route_kernel(q_ref, k_ref, ids_ref, gate_ref, sv_ref, si_ref, cv_ref, cand_ref, eid_ref, flat_ref):
    tb = q_ref.shape[0]
    nk, dk = k_ref.shape[1], k_ref.shape[2]
    kk = PEER_TOPK
    key_idx = lax.broadcasted_iota(jnp.int32, (nk, tb), 0).astype(F32)
    for p in range(2):
        s_t = lax.dot_general(k_ref[p], q_ref[:, p * dk:(p + 1) * dk], (((1,), (1,)), ((), ())),
                              precision=HIGHEST, preferred_element_type=F32)

        def emit(r, m, am, p=p):
            sv_ref[p, r:r + 1, :] = m
            si_ref[p, r:r + 1, :] = am

        _top_rows(s_t, key_idx, float(nk), kk, emit)

    sv1, si1 = sv_ref[1], si_ref[1]
    n_cand = cand_ref.shape[0]
    off = 0
    for i in range(kk):
        n_i = kk // (i + 1)
        cand_ref[off:off + n_i, :] = sv_ref[0, i:i + 1, :] + sv1[0:n_i, :]
        eid_ref[off:off + n_i, :] = si_ref[0, i:i + 1, :] * float(nk) + si1[0:n_i, :]
        flat_ref[off:off + n_i, :] = lax.broadcasted_iota(jnp.int32, (n_i, tb), 0).astype(F32) + float(i * kk)
        off += n_i
    cand_ref[off:n_cand, :] = jnp.full((n_cand - off, tb), -jnp.inf, F32)
    eid_ref[off:n_cand, :] = jnp.zeros((n_cand - off, tb), F32)
    flat_ref[off:n_cand, :] = jnp.full((n_cand - off, tb), float(kk * kk), F32)
    eid, flat = eid_ref[...], flat_ref[...]

    def emit2(r, m, am):
        e = jnp.max(jnp.where(flat == am, eid, -1.0), axis=0, keepdims=True)
        cv_ref[r:r + 1, :] = m
        ids_ref[r:r + 1, :] = e.astype(jnp.int32)

    _top_rows(cand_ref[...], flat, float(kk * kk), kk, emit2)
    cv = cv_ref[...]
    ex = jnp.exp(cv - jnp.max(cv, axis=0, keepdims=True))
    gate_ref[...] = ex / jnp.sum(ex, axis=0, keepdims=True)


def _peer_route(q, sub_keys, tb=256):
    t = q.shape[0]
    nh, _, nk, dk = sub_keys.shape
    kk = PEER_TOPK
    n_cand = sum(kk // (i + 1) for i in range(kk))
    n_cand += (-n_cand) % SUBLANES
    o_spec = pl.BlockSpec((kk, tb), lambda i, h: (h, i))
    return pl.pallas_call(
        _route_kernel,
        grid=(t // tb, nh),
        in_specs=[pl.BlockSpec((tb, 2 * dk), lambda i, h: (i, h)),
                  pl.BlockSpec((None, 2, nk, dk), lambda i, h: (h, 0, 0, 0))],
        out_specs=[o_spec, o_spec],
        out_shape=[jax.ShapeDtypeStruct((nh * kk, t), jnp.int32),
                   jax.ShapeDtypeStruct((nh * kk, t), F32)],
        scratch_shapes=[pltpu.VMEM((2, kk, tb), F32), pltpu.VMEM((2, kk, tb), F32), pltpu.VMEM((kk, tb), F32)]
        + [pltpu.VMEM((n_cand, tb), F32)] * 3,
        compiler_params=_params(("arbitrary", "arbitrary")),
        name="peer_route",
    )(q, sub_keys)


def _pack_kernel(u_ref, v_ref, o_ref):
    ub = pltpu.bitcast(u_ref[...].astype(BF16).astype(F32), jnp.uint32)
    vb = pltpu.bitcast(v_ref[...].astype(BF16).astype(F32), jnp.uint32)
    o_ref[...] = ub | (vb >> 16)


def _pack_experts(u, v, te=512):
    e, d = u.shape
    blk = pl.BlockSpec((te, d), lambda i: (i, 0))
    return pl.pallas_call(
        _pack_kernel,
        grid=(e // te,),
        in_specs=[blk, blk],
        out_specs=blk,
        out_shape=jax.ShapeDtypeStruct((e, d), jnp.uint32),
        compiler_params=_params(("arbitrary",)),
        name="pack_experts",
    )(u, v)


PEER_TOKENS_PER_STEP = 128
PEER_SLOTS = SUBLANES


def _peer_kernel(ids_ref, nids_ref, gate_ref, h_ref, x_ref, g2_ref, tab_ref, o_ref, buf_ref, orow_ref, sem_ref):
    tb, d = h_ref.shape
    ne = PEER_PER_TOKEN
    nslots = buf_ref.shape[0]
    nchunk = d // LANES
    ahead = nslots - 1
    step, last_step = pl.program_id(0), pl.num_programs(0) - 1

    def start_rows(src_ids, tok, slot):
        for k in range(ne):
            pltpu.make_async_copy(tab_ref.at[src_ids[tok, k]], buf_ref.at[slot, pl.ds(k, 1), :],
                                  sem_ref.at[slot]).start(priority=k % 2)

    def wait_token(slot):
        pltpu.make_async_copy(tab_ref.at[pl.ds(0, ne), 0], buf_ref.at[slot], sem_ref.at[slot]).wait()

    lane_tok = lax.broadcasted_iota(jnp.int32, (ne, tb), 1)

    def token(base, j, issue_next):
        wait_token(j)
        rows = pl.ds(base, nslots)
        acc = jnp.zeros((ne, LANES), F32)
        for c in range(nchunk):
            sl = slice(c * LANES, (c + 1) * LANES)
            u = pltpu.bitcast(buf_ref[j, :, sl] & jnp.uint32(0xFFFF0000), F32)
            acc = acc + u * h_ref[rows, sl][j:j + 1, :]
        act = _gelu_tanh(jnp.sum(acc, axis=1, keepdims=True))
        gcol = jnp.sum(jnp.where(lane_tok == base + j, gate_ref[...], 0.0), axis=1, keepdims=True)
        wb = jnp.broadcast_to(gcol * act, (ne, LANES))
        for c in range(nchunk):
            sl = slice(c * LANES, (c + 1) * LANES)
            v = pltpu.bitcast(buf_ref[j, :, sl] << 16, F32)
            orow_ref[j:j + 1, sl] = jnp.sum(v * wb, axis=0, keepdims=True)
        issue_next()

    def finish_group(base):
        rows = pl.ds(base, nslots)
        o_ref[rows, :] = x_ref[rows, :] + g2_ref[0] * orow_ref[...]

    @pl.when(step == 0)
    def _():
        for s in range(ahead):
            start_rows(ids_ref, s, s)

    def group(g, carry):
        base = pl.multiple_of(g * nslots, nslots)
        for j in range(nslots):
            token(base, j, functools.partial(start_rows, ids_ref, base + j + ahead, (j + ahead) % nslots))
        finish_group(base)
        return carry

    ngroups = tb // nslots
    lax.fori_loop(0, ngroups - 1, group, 0)

    base = (ngroups - 1) * nslots
    for j in range(nslots):
        nxt, nxt_slot = base + j + ahead, (j + ahead) % nslots
        if nxt < tb:
            issue_next = functools.partial(start_rows, ids_ref, nxt, nxt_slot)
        else:
            def issue_next(nxt=nxt, nxt_slot=nxt_slot):
                @pl.when(step < last_step)
                def _():
                    start_rows(nids_ref, nxt - tb, nxt_slot)
        token(base, j, issue_next)
    finish_group(base)


def _peer_experts(ids_t, gates, h, x, g2, table, rows_per_batch):
    t, d = x.shape
    tb = PEER_TOKENS_PER_STEP
    ne = PEER_PER_TOKEN
    bpb = rows_per_batch // tb
    nsteps = t // tb
    row = pl.BlockSpec((tb, d), lambda i: (i, 0))
    return pl.pallas_call(
        _peer_kernel,
        grid=(nsteps,),
        in_specs=[pl.BlockSpec((tb, ne), lambda i: (i, 0), memory_space=pltpu.SMEM),
                  pl.BlockSpec((tb, ne), lambda i: (jnp.minimum(i + 1, nsteps - 1), 0), memory_space=pltpu.SMEM),
                  pl.BlockSpec((ne, tb), lambda i: (0, i)),
                  row, row,
                  pl.BlockSpec((1, 1, d), lambda i: (i // bpb, 0, 0)),
                  pl.BlockSpec(memory_space=pl.ANY)],
        out_specs=row,
        out_shape=jax.ShapeDtypeStruct((t, d), F32),
        scratch_shapes=[pltpu.VMEM((PEER_SLOTS, ne, d), jnp.uint32),
                        pltpu.VMEM((PEER_SLOTS, d), F32),
                        pltpu.SemaphoreType.DMA((PEER_SLOTS,))],
        compiler_params=_params(("arbitrary",)),
        name="peer_experts",
    )(ids_t, ids_t, gates, h, x, g2, table.reshape(table.shape[0], 1, d))


def _rms_kernel(x_ref, g_ref, o_ref):
    x = x_ref[...]
    ms = jnp.mean(x * x, axis=-1, keepdims=True)
    o_ref[...] = (x * lax.rsqrt(ms + EPS)) * g_ref[...]


def _rms_norm(x, gamma, tm=512):
    t, d = x.shape
    row = pl.BlockSpec((tm, d), lambda i: (i, 0))
    return pl.pallas_call(
        _rms_kernel,
        grid=(t // tm,),
        in_specs=[row, pl.BlockSpec((1, d), lambda i: (0, 0))],
        out_specs=row,
        out_shape=jax.ShapeDtypeStruct((t, d), F32),
        compiler_params=_params(("arbitrary",)),
        name="final_rms_norm",
    )(x, gamma.reshape(1, d))


def _peer_layer(x, mod, gamma, w_q, sub_keys, u, v, seq):
    bsz, d = mod.shape[0], x.shape[1]
    sh2 = mod[:, 3 * d:4 * d].reshape(bsz, 1, d)
    sc2 = mod[:, 4 * d:5 * d].reshape(bsz, 1, d)
    g2 = mod[:, 5 * d:6 * d].reshape(bsz, 1, d)
    q, h = _norm_mod_matmul(x, gamma, sh2, sc2, w_q.astype(BF16), seq, 512, emit_h=True)
    ids, gates = _peer_route(q, sub_keys)
    table = _pack_experts(u, v)
    return _peer_experts(ids.T, gates, h, x, g2, table, seq)


def kernel(x, c, ctx, c_ctx, w_mod, b_mod, norm_mix_g, norm_ffn_g, norm_final_g, lru_w_in, lru_conv_w, lru_conv_b, lru_w_a, lru_b_a, lru_w_x, lru_b_x, lru_lambda, lru_w_out, sc_w_in, sc_conv_w, sc_conv_b, sc_w_out, peer_w_q, peer_sub_keys, peer_u, peer_v):
    bsz, seq, d = x.shape
    ctx_len = ctx.shape[1]
    xt = x.reshape(bsz * seq, d)

    c_rows = jnp.concatenate([c, c_ctx[None, :], jnp.zeros((SUBLANES - bsz - 1, d), F32)], axis=0)
    mods = [_mod_vectors(c_rows, w_mod[i], b_mod[i]) for i in range(w_mod.shape[0])]

    mod = mods[0][:bsz]
    sh1, sc1, g1 = (mod[:, k * d:(k + 1) * d].reshape(bsz, 1, d) for k in range(3))
    w_in16 = lru_w_in[0].astype(BF16)
    w_a16, w_x16 = lru_w_a[0].astype(BF16), lru_w_x[0].astype(BF16)
    z = _norm_mod_matmul(xt, norm_mix_g[0], sh1, sc1, w_in16, seq, 512)

    mod_c = mods[0][bsz:bsz + 1]
    sh_c = jnp.broadcast_to(mod_c[:, 0:d].reshape(1, 1, d), (bsz, 1, d))
    sc_c = jnp.broadcast_to(mod_c[:, d:2 * d].reshape(1, 1, d), (bsz, 1, d))
    z_ctx = _norm_mod_matmul(ctx.reshape(bsz * ctx_len, d), norm_mix_g[0], sh_c, sc_c, w_in16, ctx_len, ctx_len)
    coef_args = (1, lru_conv_w[0], lru_conv_b[0], w_a16, lru_b_a[0], w_x16, lru_b_x[0], lru_lambda[0])
    cf = [t.reshape(bsz, ctx_len, d) for t in _lru_coeffs(z_ctx, *coef_args, ctx_len, ctx_len)]
    zero_state = jnp.zeros((bsz, 1, d), F32)
    hc_f = _linear_scan(cf[0], cf[1], zero_state, False)
    hc_r = _linear_scan(cf[2], cf[3], zero_state, True)

    lf = [t.reshape(bsz, seq, d) for t in _lru_coeffs(z, *coef_args, GRID_W, 256)]
    h_f = _linear_scan(lf[0], lf[1], hc_f[:, ctx_len - 1:ctx_len, :], False)
    y = _linear_scan(lf[2], lf[3], hc_r[:, 0:1, :], True, prev=h_f)
    xt = _lru_out(z, y.reshape(bsz * seq, d), xt, g1, lru_w_out[0].astype(BF16), seq)
    xt = _peer_layer(xt, mod, norm_ffn_g[0], peer_w_q[0], peer_sub_keys[0], peer_u[0], peer_v[0], seq)

    mod = mods[1][:bsz]
    sh1, sc1, g1 = (mod[:, k * d:(k + 1) * d].reshape(bsz, 1, d) for k in range(3))
    z = _norm_mod_matmul(xt, norm_mix_g[1], sh1, sc1, sc_w_in[0].astype(BF16), seq, 512)
    xt = _sc_out(z, sc_conv_w[0], sc_conv_b[0], xt, g1, sc_w_out[0].astype(BF16), seq)
    xt = _peer_layer(xt, mod, norm_ffn_g[1], peer_w_q[1], peer_sub_keys[1], peer_u[1], peer_v[1], seq)

    return _rms_norm(xt, norm_final_g).reshape(bsz, seq, d)
```

```python
import functools

import jax
import jax.numpy as jnp
from jax import lax
from jax.experimental import pallas as pl
from jax.experimental.pallas import tpu as pltpu

EPS = 1e-6
GRID_W = 64
LRU_HEADS = 16
LRU_C = 8.0
PEER_HEADS = 8
PEER_N_KEYS = 128
PEER_TOPK = 16
PEER_PER_TOKEN = PEER_HEADS * PEER_TOPK

LANES = 128
SUBLANES = 8
VMEM_LIMIT = 48 * 1024 * 1024

F32 = jnp.float32
BF16 = jnp.bfloat16
HIGHEST = lax.Precision.HIGHEST


def _params(sem):
    return pltpu.CompilerParams(dimension_semantics=sem, vmem_limit_bytes=VMEM_LIMIT)


def _gelu_tanh(x):
    c = 0.7978845608028654
    return x * (0.5 * (1.0 + jnp.tanh(c * (x + 0.044715 * (x * x * x)))))


def _neg_expm1(z, exp_z):
    poly = z * (1.0 + z * (1 / 2 + z * (1 / 6 + z * (1 / 24 + z * (1 / 120 + z * (1 / 720 + z * (1 / 5040)))))))
    return jnp.where(z > -0.25, -poly, 1.0 - exp_z)


def _mod_kernel(c_ref, w_ref, b_ref, o_ref):
    c = c_ref[...]
    s = c * jax.nn.sigmoid(c)
    o_ref[...] = jnp.dot(s, w_ref[...], preferred_element_type=F32, precision=HIGHEST) + b_ref[...]


def _mod_vectors(c_rows, w, b):
    d, n = w.shape
    tn = 1024
    return pl.pallas_call(
        _mod_kernel,
        grid=(n // tn,),
        in_specs=[pl.BlockSpec((SUBLANES, d), lambda j: (0, 0)),
                  pl.BlockSpec((d, tn), lambda j: (0, j)),
                  pl.BlockSpec((1, tn), lambda j: (0, j))],
        out_specs=pl.BlockSpec((SUBLANES, tn), lambda j: (0, j)),
        out_shape=jax.ShapeDtypeStruct((SUBLANES, n), F32),
        compiler_params=_params(("arbitrary",)),
        name="mod_vectors",
    )(c_rows, w, b.reshape(1, n))


def _nmm_kernel(emit_h, x_ref, g_ref, sh_ref, sc_ref, w_ref, o_ref, *rest):
    if emit_h:
        h_ref, hn_ref = rest
    else:
        (hn_ref,) = rest

    @pl.when(pl.program_id(1) == 0)
    def _():
        x = x_ref[...]
        ms = jnp.mean(x * x, axis=-1, keepdims=True)
        y = (x * lax.rsqrt(ms + EPS)) * g_ref[...]
        h = y * (1.0 + sc_ref[0]) + sh_ref[0]
        hn_ref[...] = h.astype(BF16)
        if emit_h:
            h_ref[...] = h

    o_ref[...] = jnp.dot(hn_ref[...], w_ref[...], preferred_element_type=F32)


def _norm_mod_matmul(x, gamma, shift, scale, w_bf16, rows_per_batch, emit_h=False):
    t, d = x.shape
    n = w_bf16.shape[1]
    tn = 1024
    tm = min(512 if emit_h else 1024, rows_per_batch)
    bpb = rows_per_batch // tm
    out_shape = [jax.ShapeDtypeStruct((t, n), F32)]
    out_specs = [pl.BlockSpec((tm, tn), lambda i, j: (i, j))]
    if emit_h:
        out_shape.append(jax.ShapeDtypeStruct((t, d), F32))
        out_specs.append(pl.BlockSpec((tm, d), lambda i, j: (i, 0)))
    res = pl.pallas_call(
        functools.partial(_nmm_kernel, emit_h),
        grid=(t // tm, n // tn),
        in_specs=[pl.BlockSpec((tm, d), lambda i, j: (i, 0)),
                  pl.BlockSpec((1, d), lambda i, j: (0, 0)),
                  pl.BlockSpec((1, 1, d), lambda i, j: (i // bpb, 0, 0)),
                  pl.BlockSpec((1, 1, d), lambda i, j: (i // bpb, 0, 0)),
                  pl.BlockSpec((d, tn), lambda i, j: (0, j))],
        out_specs=out_specs,
        out_shape=out_shape,
        scratch_shapes=[pltpu.VMEM((tm, d), BF16)],
        compiler_params=_params(("arbitrary", "arbitrary")),
        name="norm_mod_matmul",
    )(x, gamma.reshape(1, d), shift, scale, w_bf16)
    return res if emit_h else res[0]


def _row_conv(u, taps, bias, pad_left, row_len):
    tm = u.shape[0]
    pos = lax.broadcasted_iota(jnp.int32, (tm, 1), 0) % row_len
    y = None
    for k in range(len(taps)):
        off = k - pad_left
        if off == 0:
            term = u
        else:
            rolled = pltpu.roll(u, (-off) % tm, axis=0)
            valid = jnp.logical_and(pos + off >= 0, pos + off < row_len)
            term = jnp.where(valid, rolled, 0.0)
        y = (bias + taps[k] * term) if y is None else (y + taps[k] * term)
    return y


def _lru_coef_kernel(row_len, xb_ref, cw_ref, cb_ref, wa_ref, ba_ref, wx_ref, bx_ref, lam_ref,
                     af_ref, bf_ref, ar_ref, br_ref):
    taps = [cw_ref[k:k + 1, :] for k in range(cw_ref.shape[0])]
    xc = _row_conv(xb_ref[...], taps, cb_ref[...], 2, row_len)
    d = xc.shape[1]
    hd = d // LRU_HEADS
    outs = ((af_ref, bf_ref), (ar_ref, br_ref))
    for dr in range(2):
        lam = lam_ref[dr:dr + 1, :]
        z = -lam
        sp = jnp.maximum(z, 0.0) + jnp.log1p(jnp.exp(-jnp.abs(z)))
        a_ref, b_ref = outs[dr]
        for h in range(LRU_HEADS):
            sl = slice(h * hd, (h + 1) * hd)
            xh = xc[:, sl]
            xh16 = xh.astype(BF16)
            r = jax.nn.sigmoid(jnp.dot(xh16, wa_ref[dr, h], preferred_element_type=F32) + ba_ref[dr:dr + 1, sl])
            i = jax.nn.sigmoid(jnp.dot(xh16, wx_ref[dr, h], preferred_element_type=F32) + bx_ref[dr:dr + 1, sl])
            log_a = (-LRU_C) * r * sp[:, sl]
            a = jnp.exp(log_a)
            a_ref[:, sl] = a
            b_ref[:, sl] = jnp.sqrt(_neg_expm1(2.0 * log_a, a * a)) * (i * xh)


def _lru_coeffs(z, col_block, conv_w, conv_b, w_a16, b_a, w_x16, b_x, lam, row_len, tm):
    t = z.shape[0]
    d = conv_w.shape[1]
    full = lambda *s: pl.BlockSpec(s, lambda i: (0,) * len(s))
    o_spec = pl.BlockSpec((tm, d), lambda i: (i, 0))
    o_shape = jax.ShapeDtypeStruct((t, d), F32)
    return pl.pallas_call(
        functools.partial(_lru_coef_kernel, row_len),
        grid=(t // tm,),
        in_specs=[pl.BlockSpec((tm, d), lambda i: (i, col_block)),
                  full(*conv_w.shape), full(1, d),
                  full(*w_a16.shape), full(2, d), full(*w_x16.shape), full(2, d), full(2, d)],
        out_specs=[o_spec] * 4,
        out_shape=[o_shape] * 4,
        compiler_params=_params(("arbitrary",)),
        name="lru_coeffs",
    )(z, conv_w, conv_b.reshape(1, d), w_a16, b_a, w_x16, b_x, lam)


def _scan_kernel(reverse, add_prev, a_ref, b_ref, h0_ref, *rest):
    if add_prev:
        y_ref, o_ref, st_ref = rest
    else:
        o_ref, st_ref = rest
    ts, cw = a_ref.shape
    ng = ts // SUBLANES

    @pl.when(pl.program_id(2) == 0)
    def _():
        st_ref[...] = jnp.broadcast_to(h0_ref[...], st_ref.shape)

    row = lax.broadcasted_iota(jnp.int32, (SUBLANES, cw), 0)

    def body(gi, h):
        g = (ng - 1 - gi) if reverse else gi
        off = pl.multiple_of(g * SUBLANES, SUBLANES)
        a = a_ref[pl.ds(off, SUBLANES), :]
        b = b_ref[pl.ds(off, SUBLANES), :]
        for s in (1, 2, 4):
            if reverse:
                shift, m = SUBLANES - s, row < SUBLANES - s
            else:
                shift, m = s, row >= s
            a_s = jnp.where(m, pltpu.roll(a, shift, axis=0), 1.0)
            b_s = jnp.where(m, pltpu.roll(b, shift, axis=0), 0.0)
            b = a * b_s + b
            a = a * a_s
        hh = a * h + b
        if add_prev:
            o_ref[pl.ds(off, SUBLANES), :] = hh + y_ref[pl.ds(off, SUBLANES), :]
        else:
            o_ref[pl.ds(off, SUBLANES), :] = hh
        last = hh[0:1, :] if reverse else hh[SUBLANES - 1:SUBLANES, :]
        return jnp.broadcast_to(last, (SUBLANES, cw))

    st_ref[...] = lax.fori_loop(0, ng, body, st_ref[...], unroll=2)


def _linear_scan(a, b, h0, reverse, prev=None, ts=512, cw=512):
    bsz, s, c = a.shape
    ts = min(ts, s)
    ns = s // ts
    tmap = (lambda bb, cc, j: (bb, ns - 1 - j, cc)) if reverse else (lambda bb, cc, j: (bb, j, cc))
    blk = pl.BlockSpec((None, ts, cw), tmap)
    ins = [a, b, h0]
    in_specs = [blk, blk, pl.BlockSpec((None, 1, cw), lambda bb, cc, j: (bb, 0, cc))]
    if prev is not None:
        ins.append(prev)
        in_specs.append(blk)
    return pl.pallas_call(
        functools.partial(_scan_kernel, reverse, prev is not None),
        grid=(bsz, c // cw, ns),
        in_specs=in_specs,
        out_specs=blk,
        out_shape=jax.ShapeDtypeStruct((bsz, s, c), F32),
        scratch_shapes=[pltpu.VMEM((SUBLANES, cw), F32)],
        compiler_params=_params(("arbitrary", "arbitrary", "arbitrary")),
        name="linear_scan",
    )(*ins)


def _lru_out_kernel(gate_ref, y_ref, x_ref, g1_ref, w_ref, o_ref):
    p = _gelu_tanh(gate_ref[...]) * y_ref[...]
    out = jnp.dot(p.astype(BF16), w_ref[...], preferred_element_type=F32)
    o_ref[...] = x_ref[...] + g1_ref[0] * out


def _lru_out(z, y, x, g1, w16, rows_per_batch, tm=256):
    t, d = x.shape
    bpb = rows_per_batch // tm
    row = pl.BlockSpec((tm, d), lambda i: (i, 0))
    return pl.pallas_call(
        _lru_out_kernel,
        grid=(t // tm,),
        in_specs=[row, row, row,
                  pl.BlockSpec((1, 1, d), lambda i: (i // bpb, 0, 0)),
                  pl.BlockSpec((d, d), lambda i: (0, 0))],
        out_specs=row,
        out_shape=jax.ShapeDtypeStruct((t, d), F32),
        compiler_params=_params(("arbitrary",)),
        name="lru_out",
    )(z, y, x, g1, w16)


def _sc_out_kernel(bg_ref, cg_ref, v_ref, cw_ref, cb_ref, x_ref, g1_ref, w_ref, o_ref):
    taps = [cw_ref[k:k + 1, :] for k in range(cw_ref.shape[0])]
    y = _row_conv(cg_ref[...] * v_ref[...], taps, cb_ref[...], 1, GRID_W)
    p = bg_ref[...] * y
    out = jnp.dot(p.astype(BF16), w_ref[...], preferred_element_type=F32)
    o_ref[...] = x_ref[...] + g1_ref[0] * out


def _sc_out(z, conv_w, conv_b, x, g1, w16, rows_per_batch, tm=256):
    t, d = x.shape
    bpb = rows_per_batch // tm
    row = pl.BlockSpec((tm, d), lambda i: (i, 0))
    col = lambda c: pl.BlockSpec((tm, d), lambda i: (i, c))
    return pl.pallas_call(
        _sc_out_kernel,
        grid=(t // tm,),
        in_specs=[col(0), col(1), col(2),
                  pl.BlockSpec(conv_w.shape, lambda i: (0, 0)),
                  pl.BlockSpec((1, d), lambda i: (0, 0)),
                  row,
                  pl.BlockSpec((1, 1, d), lambda i: (i // bpb, 0, 0)),
                  pl.BlockSpec((d, d), lambda i: (0, 0))],
        out_specs=row,
        out_shape=jax.ShapeDtypeStruct((t, d), F32),
        compiler_params=_params(("arbitrary",)),
        name="sc_out",
    )(z, z, z, conv_w, conv_b.reshape(1, d), x, g1, w16)


def _top_rows(vals, idx_f, fill, n_out, emit):
    for r in range(n_out):
        m = jnp.max(vals, axis=0, keepdims=True)
        am = jnp.min(jnp.where(vals == m, idx_f, fill), axis=0, keepdims=True)
        emit(r, m, am)
        vals = jnp.where(idx_f == am, -jnp.inf, vals)


def _route_kernel(q_ref, k_ref, ids_ref, gate_ref, sv_ref, si_ref, cv_ref, cand_ref, eid_ref, flat_ref):
    tb = q_ref.shape[0]
    nk, dk = k_ref.shape[1], k_ref.shape[2]
    kk = PEER_TOPK
    key_idx = lax.broadcasted_iota(jnp.int32, (nk, tb), 0).astype(F32)
    for p in range(2):
        s_t = lax.dot_general(k_ref[p], q_ref[:, p * dk:(p + 1) * dk], (((1,), (1,)), ((), ())),
                              precision=HIGHEST, preferred_element_type=F32)

        def emit(r, m, am, p=p):
            sv_ref[p, r:r + 1, :] = m
            si_ref[p, r:r + 1, :] = am

        _top_rows(s_t, key_idx, float(nk), kk, emit)

    sv1, si1 = sv_ref[1], si_ref[1]
    n_cand = cand_ref.shape[0]
    off = 0
    for i in range(kk):
        n_i = kk // (i + 1)
        cand_ref[off:off + n_i, :] = sv_ref[0, i:i + 1, :] + sv1[0:n_i, :]
        eid_ref[off:off + n_i, :] = si_ref[0, i:i + 1, :] * float(nk) + si1[0:n_i, :]
        flat_ref[off:off + n_i, :] = lax.broadcasted_iota(jnp.int32, (n_i, tb), 0).astype(F32) + float(i * kk)
        off += n_i
    cand_ref[off:n_cand, :] = jnp.full((n_cand - off, tb), -jnp.inf, F32)
    eid_ref[off:n_cand, :] = jnp.zeros((n_cand - off, tb), F32)
    flat_ref[off:n_cand, :] = jnp.full((n_cand - off, tb), float(kk * kk), F32)
    eid, flat = eid_ref[...], flat_ref[...]

    def emit2(r, m, am):
        e = jnp.max(jnp.where(flat == am, eid, -1.0), axis=0, keepdims=True)
        cv_ref[r:r + 1, :] = m
        ids_ref[r:r + 1, :] = e.astype(jnp.int32)

    _top_rows(cand_ref[...], flat, float(kk * kk), kk, emit2)
    cv = cv_ref[...]
    ex = jnp.exp(cv - jnp.max(cv, axis=0, keepdims=True))
    gate_ref[...] = ex / jnp.sum(ex, axis=0, keepdims=True)


def _peer_route(q, sub_keys, tb=256):
    t = q.shape[0]
    nh, _, nk, dk = sub_keys.shape
    kk = PEER_TOPK
    n_cand = sum(kk // (i + 1) for i in range(kk))
    n_cand += (-n_cand) % SUBLANES
    o_spec = pl.BlockSpec((kk, tb), lambda i, h: (h, i))
    return pl.pallas_call(
        _route_kernel,
        grid=(t // tb, nh),
        in_specs=[pl.BlockSpec((tb, 2 * dk), lambda i, h: (i, h)),
                  pl.BlockSpec((None, 2, nk, dk), lambda i, h: (h, 0, 0, 0))],
        out_specs=[o_spec, o_spec],
        out_shape=[jax.ShapeDtypeStruct((nh * kk, t), jnp.int32),
                   jax.ShapeDtypeStruct((nh * kk, t), F32)],
        scratch_shapes=[pltpu.VMEM((2, kk, tb), F32), pltpu.VMEM((2, kk, tb), F32), pltpu.VMEM((kk, tb), F32)]
        + [pltpu.VMEM((n_cand, tb), F32)] * 3,
        compiler_params=_params(("arbitrary", "arbitrary")),
        name="peer_route",
    )(q, sub_keys)


def _pack_pair(x):
    half = x.shape[1] // 2
    bits = pltpu.bitcast(x.astype(BF16).astype(F32), jnp.uint32)
    return bits[:, half:] | (bits[:, :half] >> 16)


def _unpack_pair(w):
    return pltpu.bitcast(w << 16, F32), pltpu.bitcast(w & jnp.uint32(0xFFFF0000), F32)


def _pack_kernel(u_ref, v_ref, o_ref):
    half = u_ref.shape[1] // 2
    o_ref[:, :half] = _pack_pair(u_ref[...])
    o_ref[:, half:] = _pack_pair(v_ref[...])


def _pack_experts(u, v, te=512):
    e, d = u.shape
    blk = pl.BlockSpec((te, d), lambda i: (i, 0))
    return pl.pallas_call(
        _pack_kernel,
        grid=(e // te,),
        in_specs=[blk, blk],
        out_specs=blk,
        out_shape=jax.ShapeDtypeStruct((e, d), jnp.uint32),
        compiler_params=_params(("arbitrary",)),
        name="pack_experts",
    )(u, v)


PEER_TOKENS_PER_STEP = 128
PEER_SLOTS = SUBLANES


def _peer_kernel(final_norm, head_ref, ids_ref, gate_ref, h_ref, x_ref, g2_ref, fin_ref, tab_ref, o_ref,
                 buf_ref, orow_ref, sem_ref):
    tb, d = h_ref.shape
    ne = PEER_PER_TOKEN
    nslots = buf_ref.shape[0]
    nword = d // (2 * LANES)
    ahead = nslots - 1
    step, last_step = pl.program_id(0), pl.num_programs(0) - 1

    def start_rows(src_ids, tok, slot, part=0, nparts=1):
        for k in range(part * ne // nparts, (part + 1) * ne // nparts):
            pltpu.make_async_copy(tab_ref.at[src_ids[tok, k]], buf_ref.at[slot, pl.ds(k, 1), :],
                                  sem_ref.at[slot]).start(priority=k % 2)

    def wait_token(slot):
        pltpu.make_async_copy(tab_ref.at[pl.ds(0, ne), 0], buf_ref.at[slot], sem_ref.at[slot]).wait()

    lane_tok = lax.broadcasted_iota(jnp.int32, (ne, tb), 1)

    def token(base, j, issue_next):
        wait_token(j)
        rows = pl.ds(base, nslots)
        acc = jnp.zeros((ne, LANES), F32)
        for c in range(nword):
            lo, hi = _unpack_pair(buf_ref[j, :, c * LANES:(c + 1) * LANES])
            acc = acc + lo * h_ref[rows, c * LANES:(c + 1) * LANES][j:j + 1, :]
            acc = acc + hi * h_ref[rows, (nword + c) * LANES:(nword + c + 1) * LANES][j:j + 1, :]
            issue_next(c, 2 * nword)
        act = _gelu_tanh(jnp.sum(acc, axis=1, keepdims=True))
        gcol = jnp.sum(jnp.where(lane_tok == base + j, gate_ref[...], 0.0), axis=1, keepdims=True)
        wb = jnp.broadcast_to(gcol * act, (ne, LANES))
        for c in range(nword):
            lo, hi = _unpack_pair(buf_ref[j, :, (nword + c) * LANES:(nword + c + 1) * LANES])
            orow_ref[j:j + 1, c * LANES:(c + 1) * LANES] = jnp.sum(lo * wb, axis=0, keepdims=True)
            orow_ref[j:j + 1, (nword + c) * LANES:(nword + c + 1) * LANES] = jnp.sum(hi * wb, axis=0, keepdims=True)
            issue_next(nword + c, 2 * nword)

    def finish_group(base):
        rows = pl.ds(base, nslots)
        y = x_ref[rows, :] + g2_ref[0] * orow_ref[...]
        if final_norm:
            y = (y * lax.rsqrt(jnp.mean(y * y, axis=-1, keepdims=True) + EPS)) * fin_ref[...]
        o_ref[rows, :] = y

    @pl.when(step == 0)
    def _():
        def first(s, carry):
            start_rows(head_ref, s, s)
            return carry
        lax.fori_loop(0, ahead, first, 0)

    def group(g, carry):
        base = pl.multiple_of(g * nslots, nslots)
        for j in range(nslots):
            token(base, j, functools.partial(start_rows, ids_ref, base + j, (j + ahead) % nslots))
        finish_group(base)
        return carry

    lax.fori_loop(0, tb // nslots, group, 0)

    @pl.when(step == last_step)
    def _():
        for s in range(ahead):
            wait_token(s)


def _peer_experts(ids_t, gates, h, x, g2, table, rows_per_batch, final_gamma, final_norm):
    t, d = x.shape
    tb = PEER_TOKENS_PER_STEP
    ne = PEER_PER_TOKEN
    bpb = rows_per_batch // tb
    ahead = PEER_SLOTS - 1
    ids_ahead = jnp.concatenate([ids_t[ahead:], jnp.zeros((ahead, ne), jnp.int32)], axis=0)
    row = pl.BlockSpec((tb, d), lambda i: (i, 0))
    return pl.pallas_call(
        functools.partial(_peer_kernel, final_norm),
        grid=(t // tb,),
        in_specs=[pl.BlockSpec((PEER_SLOTS, ne), lambda i: (0, 0), memory_space=pltpu.SMEM),
                  pl.BlockSpec((tb, ne), lambda i: (i, 0), memory_space=pltpu.SMEM),
                  pl.BlockSpec((ne, tb), lambda i: (0, i)),
                  row, row,
                  pl.BlockSpec((1, 1, d), lambda i: (i // bpb, 0, 0)),
                  pl.BlockSpec((1, d), lambda i: (0, 0)),
                  pl.BlockSpec(memory_space=pl.ANY)],
        out_specs=row,
        out_shape=jax.ShapeDtypeStruct((t, d), F32),
        scratch_shapes=[pltpu.VMEM((PEER_SLOTS, ne, d), jnp.uint32),
                        pltpu.VMEM((PEER_SLOTS, d), F32),
                        pltpu.SemaphoreType.DMA((PEER_SLOTS,))],
        compiler_params=_params(("arbitrary",)),
        name="peer_experts",
    )(ids_t[:PEER_SLOTS], ids_ahead, gates, h, x, g2, final_gamma.reshape(1, d),
      table.reshape(table.shape[0], 1, d))


def _peer_layer(x, mod, gamma, w_q, sub_keys, u, v, seq, final_gamma, final_norm):
    bsz, d = mod.shape[0], x.shape[1]
    sh2 = mod[:, 3 * d:4 * d].reshape(bsz, 1, d)
    sc2 = mod[:, 4 * d:5 * d].reshape(bsz, 1, d)
    g2 = mod[:, 5 * d:6 * d].reshape(bsz, 1, d)
    q, h = _norm_mod_matmul(x, gamma, sh2, sc2, w_q.astype(BF16), seq, emit_h=True)
    ids, gates = _peer_route(q, sub_keys)
    table = _pack_experts(u, v)
    return _peer_experts(ids.T, gates, h, x, g2, table, seq, final_gamma, final_norm)


def kernel(x, c, ctx, c_ctx, w_mod, b_mod, norm_mix_g, norm_ffn_g, norm_final_g, lru_w_in, lru_conv_w, lru_conv_b, lru_w_a, lru_b_a, lru_w_x, lru_b_x, lru_lambda, lru_w_out, sc_w_in, sc_conv_w, sc_conv_b, sc_w_out, peer_w_q, peer_sub_keys, peer_u, peer_v):
    bsz, seq, d = x.shape
    ctx_len = ctx.shape[1]
    xt = x.reshape(bsz * seq, d)

    c_rows = jnp.concatenate([c, c_ctx[None, :], jnp.zeros((SUBLANES - bsz - 1, d), F32)], axis=0)
    mods = [_mod_vectors(c_rows, w_mod[i], b_mod[i]) for i in range(w_mod.shape[0])]

    mod = mods[0][:bsz]
    sh1, sc1, g1 = (mod[:, k * d:(k + 1) * d].reshape(bsz, 1, d) for k in range(3))
    w_in16 = lru_w_in[0].astype(BF16)
    w_a16, w_x16 = lru_w_a[0].astype(BF16), lru_w_x[0].astype(BF16)
    z = _norm_mod_matmul(xt, norm_mix_g[0], sh1, sc1, w_in16, seq)

    mod_c = mods[0][bsz:bsz + 1]
    sh_c = jnp.broadcast_to(mod_c[:, 0:d].reshape(1, 1, d), (bsz, 1, d))
    sc_c = jnp.broadcast_to(mod_c[:, d:2 * d].reshape(1, 1, d), (bsz, 1, d))
    z_ctx = _norm_mod_matmul(ctx.reshape(bsz * ctx_len, d), norm_mix_g[0], sh_c, sc_c, w_in16, ctx_len)
    coef_args = (1, lru_conv_w[0], lru_conv_b[0], w_a16, lru_b_a[0], w_x16, lru_b_x[0], lru_lambda[0])
    cf = [t.reshape(bsz, ctx_len, d) for t in _lru_coeffs(z_ctx, *coef_args, ctx_len, ctx_len)]
    zero_state = jnp.zeros((bsz, 1, d), F32)
    hc_f = _linear_scan(cf[0], cf[1], zero_state, False)
    hc_r = _linear_scan(cf[2], cf[3], zero_state, True)

    lf = [t.reshape(bsz, seq, d) for t in _lru_coeffs(z, *coef_args, GRID_W, 256)]
    h_f = _linear_scan(lf[0], lf[1], hc_f[:, ctx_len - 1:ctx_len, :], False)
    y = _linear_scan(lf[2], lf[3], hc_r[:, 0:1, :], True, prev=h_f)
    xt = _lru_out(z, y.reshape(bsz * seq, d), xt, g1, lru_w_out[0].astype(BF16), seq)
    xt = _peer_layer(xt, mod, norm_ffn_g[0], peer_w_q[0], peer_sub_keys[0], peer_u[0], peer_v[0], seq,
                     norm_final_g, False)

    mod = mods[1][:bsz]
    sh1, sc1, g1 = (mod[:, k * d:(k + 1) * d].reshape(bsz, 1, d) for k in range(3))
    z = _norm_mod_matmul(xt, norm_mix_g[1], sh1, sc1, sc_w_in[0].astype(BF16), seq)
    xt = _sc_out(z, sc_conv_w[0], sc_conv_b[0], xt, g1, sc_w_out[0].astype(BF16), seq)
    xt = _peer_layer(xt, mod, norm_ffn_g[1], peer_w_q[1], peer_sub_keys[1], peer_u[1], peer_v[1], seq,
                     norm_final_g, True)
    return xt.reshape(bsz, seq, d)
```

```python
import functools

import jax
import jax.numpy as jnp
from jax import lax
from jax.experimental import pallas as pl
from jax.experimental.pallas import tpu as pltpu

EPS = 1e-6
GRID_W = 64
LRU_HEADS = 16
LRU_C = 8.0
PEER_HEADS = 8
PEER_N_KEYS = 128
PEER_TOPK = 16
PEER_PER_TOKEN = PEER_HEADS * PEER_TOPK

LANES = 128
SUBLANES = 8
VMEM_LIMIT = 48 * 1024 * 1024

F32 = jnp.float32
BF16 = jnp.bfloat16
HIGHEST = lax.Precision.HIGHEST


def _params(sem):
    return pltpu.CompilerParams(dimension_semantics=sem, vmem_limit_bytes=VMEM_LIMIT)


def _gelu_tanh(x):
    c = 0.7978845608028654
    return x * (0.5 * (1.0 + jnp.tanh(c * (x + 0.044715 * (x * x * x)))))


def _neg_expm1_2x(half_z, exp_z):
    return -jnp.tanh(half_z) * (exp_z + 1.0)


def _sigmoid(x):
    return 0.5 * (jnp.tanh(0.5 * x) + 1.0)


def _mod_kernel(c_ref, w_ref, b_ref, o_ref):
    c = c_ref[...]
    s = c * jax.nn.sigmoid(c)
    o_ref[...] = jnp.dot(s, w_ref[...], preferred_element_type=F32, precision=HIGHEST) + b_ref[...]


def _mod_vectors(c_rows, w, b):
    d, n = w.shape
    tn = 1024
    return pl.pallas_call(
        _mod_kernel,
        grid=(n // tn,),
        in_specs=[pl.BlockSpec((SUBLANES, d), lambda j: (0, 0)),
                  pl.BlockSpec((d, tn), lambda j: (0, j)),
                  pl.BlockSpec((1, tn), lambda j: (0, j))],
        out_specs=pl.BlockSpec((SUBLANES, tn), lambda j: (0, j)),
        out_shape=jax.ShapeDtypeStruct((SUBLANES, n), F32),
        compiler_params=_params(("arbitrary",)),
        name="mod_vectors",
    )(c_rows, w, b.reshape(1, n))


def _nmm_kernel(emit_h, x_ref, g_ref, sh_ref, sc_ref, w_ref, o_ref, *rest):
    if emit_h:
        h_ref, hn_ref = rest
    else:
        (hn_ref,) = rest

    @pl.when(pl.program_id(1) == 0)
    def _():
        x = x_ref[...]
        ms = jnp.mean(x * x, axis=-1, keepdims=True)
        y = (x * lax.rsqrt(ms + EPS)) * g_ref[...]
        h = y * (1.0 + sc_ref[0]) + sh_ref[0]
        hn_ref[...] = h.astype(BF16)
        if emit_h:
            h_ref[...] = h

    o_ref[...] = jnp.dot(hn_ref[...], w_ref[...], preferred_element_type=F32).astype(o_ref.dtype)


def _norm_mod_matmul(x, gamma, shift, scale, w_bf16, rows_per_batch, emit_h=False, out_dtype=F32):
    t, d = x.shape
    n = w_bf16.shape[1]
    tn = 1024
    tm = min(512 if emit_h else 1024, rows_per_batch)
    bpb = rows_per_batch // tm
    out_shape = [jax.ShapeDtypeStruct((t, n), out_dtype)]
    out_specs = [pl.BlockSpec((tm, tn), lambda i, j: (i, j))]
    if emit_h:
        out_shape.append(jax.ShapeDtypeStruct((t, d), F32))
        out_specs.append(pl.BlockSpec((tm, d), lambda i, j: (i, 0)))
    res = pl.pallas_call(
        functools.partial(_nmm_kernel, emit_h),
        grid=(t // tm, n // tn),
        in_specs=[pl.BlockSpec((tm, d), lambda i, j: (i, 0)),
                  pl.BlockSpec((1, d), lambda i, j: (0, 0)),
                  pl.BlockSpec((1, 1, d), lambda i, j: (i // bpb, 0, 0)),
                  pl.BlockSpec((1, 1, d), lambda i, j: (i // bpb, 0, 0)),
                  pl.BlockSpec((d, tn), lambda i, j: (0, j))],
        out_specs=out_specs,
        out_shape=out_shape,
        scratch_shapes=[pltpu.VMEM((tm, d), BF16)],
        compiler_params=_params(("arbitrary", "arbitrary")),
        name="norm_mod_matmul",
    )(x, gamma.reshape(1, d), shift, scale, w_bf16)
    return res if emit_h else res[0]


def _row_conv(u, taps, bias, pad_left, row_len):
    tm = u.shape[0]
    pos = lax.broadcasted_iota(jnp.int32, (tm, 1), 0) % row_len
    y = None
    for k in range(len(taps)):
        off = k - pad_left
        if off == 0:
            term = u
        else:
            rolled = pltpu.roll(u, (-off) % tm, axis=0)
            valid = jnp.logical_and(pos + off >= 0, pos + off < row_len)
            term = jnp.where(valid, rolled, 0.0)
        y = (bias + taps[k] * term) if y is None else (y + taps[k] * term)
    return y


def _lru_coef_kernel(row_len, xb_ref, cw_ref, cb_ref, wa_ref, ba_ref, wx_ref, bx_ref, lam_ref,
                     af_ref, bf_ref, ar_ref, br_ref):
    taps = [cw_ref[k:k + 1, :] for k in range(cw_ref.shape[0])]
    xc = _row_conv(xb_ref[...].astype(F32), taps, cb_ref[...], 2, row_len)
    d = xc.shape[1]
    hd = d // LRU_HEADS
    outs = ((af_ref, bf_ref), (ar_ref, br_ref))
    for dr in range(2):
        lam = lam_ref[dr:dr + 1, :]
        z = -lam
        nsp = (-LRU_C) * (jnp.maximum(z, 0.0) + jnp.log1p(jnp.exp(-jnp.abs(z))))
        a_ref, b_ref = outs[dr]
        for h in range(LRU_HEADS):
            sl = slice(h * hd, (h + 1) * hd)
            xh = xc[:, sl]
            xh16 = xh.astype(BF16)
            r = _sigmoid(jnp.dot(xh16, wa_ref[dr, h], preferred_element_type=F32) + ba_ref[dr:dr + 1, sl])
            i = _sigmoid(jnp.dot(xh16, wx_ref[dr, h], preferred_element_type=F32) + bx_ref[dr:dr + 1, sl])
            log_a = r * nsp[:, sl]
            a = jnp.exp(log_a)
            a_ref[:, sl] = a
            b_ref[:, sl] = jnp.sqrt(_neg_expm1_2x(log_a, a * a)) * (i * xh)


def _lru_coeffs(z, col_block, conv_w, conv_b, w_a16, b_a, w_x16, b_x, lam, row_len, tm):
    t = z.shape[0]
    d = conv_w.shape[1]
    full = lambda *s: pl.BlockSpec(s, lambda i: (0,) * len(s))
    o_spec = pl.BlockSpec((tm, d), lambda i: (i, 0))
    o_shape = jax.ShapeDtypeStruct((t, d), F32)
    return pl.pallas_call(
        functools.partial(_lru_coef_kernel, row_len),
        grid=(t // tm,),
        in_specs=[pl.BlockSpec((tm, d), lambda i: (i, col_block)),
                  full(*conv_w.shape), full(1, d),
                  full(*w_a16.shape), full(2, d), full(*w_x16.shape), full(2, d), full(2, d)],
        out_specs=[o_spec] * 4,
        out_shape=[o_shape] * 4,
        compiler_params=_params(("arbitrary",)),
        name="lru_coeffs",
    )(z, conv_w, conv_b.reshape(1, d), w_a16, b_a, w_x16, b_x, lam)


def _scan_kernel(reverse, add_prev, a_ref, b_ref, h0_ref, *rest):
    if add_prev:
        y_ref, o_ref, st_ref = rest
    else:
        o_ref, st_ref = rest
    ts, cw = a_ref.shape
    ng = ts // SUBLANES

    @pl.when(pl.program_id(2) == 0)
    def _():
        st_ref[...] = jnp.broadcast_to(h0_ref[...], st_ref.shape)

    row = lax.broadcasted_iota(jnp.int32, (SUBLANES, cw), 0)

    def body(gi, h):
        g = (ng - 1 - gi) if reverse else gi
        off = pl.multiple_of(g * SUBLANES, SUBLANES)
        a = a_ref[pl.ds(off, SUBLANES), :]
        b = b_ref[pl.ds(off, SUBLANES), :]
        for s in (1, 2, 4):
            if reverse:
                shift, m = SUBLANES - s, row < SUBLANES - s
            else:
                shift, m = s, row >= s
            a_s = jnp.where(m, pltpu.roll(a, shift, axis=0), 1.0)
            b_s = jnp.where(m, pltpu.roll(b, shift, axis=0), 0.0)
            b = a * b_s + b
            a = a * a_s
        hh = a * h + b
        if add_prev:
            o_ref[pl.ds(off, SUBLANES), :] = hh + y_ref[pl.ds(off, SUBLANES), :]
        else:
            o_ref[pl.ds(off, SUBLANES), :] = hh
        last = hh[0:1, :] if reverse else hh[SUBLANES - 1:SUBLANES, :]
        return jnp.broadcast_to(last, (SUBLANES, cw))

    st_ref[...] = lax.fori_loop(0, ng, body, st_ref[...], unroll=2)


def _linear_scan(a, b, h0, reverse, prev=None, ts=512, cw=512):
    bsz, s, c = a.shape
    ts = min(ts, s)
    ns = s // ts
    tmap = (lambda bb, cc, j: (bb, ns - 1 - j, cc)) if reverse else (lambda bb, cc, j: (bb, j, cc))
    blk = pl.BlockSpec((None, ts, cw), tmap)
    ins = [a, b, h0]
    in_specs = [blk, blk, pl.BlockSpec((None, 1, cw), lambda bb, cc, j: (bb, 0, cc))]
    if prev is not None:
        ins.append(prev)
        in_specs.append(blk)
    return pl.pallas_call(
        functools.partial(_scan_kernel, reverse, prev is not None),
        grid=(bsz, c // cw, ns),
        in_specs=in_specs,
        out_specs=blk,
        out_shape=jax.ShapeDtypeStruct((bsz, s, c), F32),
        scratch_shapes=[pltpu.VMEM((SUBLANES, cw), F32)],
        compiler_params=_params(("arbitrary", "arbitrary", "arbitrary")),
        name="linear_scan",
    )(*ins)


def _lru_out_kernel(gate_ref, y_ref, x_ref, g1_ref, w_ref, o_ref):
    p = _gelu_tanh(gate_ref[...].astype(F32)) * y_ref[...]
    out = jnp.dot(p.astype(BF16), w_ref[...], preferred_element_type=F32)
    o_ref[...] = x_ref[...] + g1_ref[0] * out


def _lru_out(z, y, x, g1, w16, rows_per_batch, tm=256):
    t, d = x.shape
    bpb = rows_per_batch // tm
    row = pl.BlockSpec((tm, d), lambda i: (i, 0))
    return pl.pallas_call(
        _lru_out_kernel,
        grid=(t // tm,),
        in_specs=[row, row, row,
                  pl.BlockSpec((1, 1, d), lambda i: (i // bpb, 0, 0)),
                  pl.BlockSpec((d, d), lambda i: (0, 0))],
        out_specs=row,
        out_shape=jax.ShapeDtypeStruct((t, d), F32),
        compiler_params=_params(("arbitrary",)),
        name="lru_out",
    )(z, y, x, g1, w16)


def _sc_out_kernel(bg_ref, cg_ref, v_ref, cw_ref, cb_ref, x_ref, g1_ref, w_ref, o_ref):
    taps = [cw_ref[k:k + 1, :] for k in range(cw_ref.shape[0])]
    y = _row_conv(cg_ref[...].astype(F32) * v_ref[...].astype(F32), taps, cb_ref[...], 1, GRID_W)
    p = bg_ref[...].astype(F32) * y
    out = jnp.dot(p.astype(BF16), w_ref[...], preferred_element_type=F32)
    o_ref[...] = x_ref[...] + g1_ref[0] * out


def _sc_out(z, conv_w, conv_b, x, g1, w16, rows_per_batch, tm=256):
    t, d = x.shape
    bpb = rows_per_batch // tm
    row = pl.BlockSpec((tm, d), lambda i: (i, 0))
    col = lambda c: pl.BlockSpec((tm, d), lambda i: (i, c))
    return pl.pallas_call(
        _sc_out_kernel,
        grid=(t // tm,),
        in_specs=[col(0), col(1), col(2),
                  pl.BlockSpec(conv_w.shape, lambda i: (0, 0)),
                  pl.BlockSpec((1, d), lambda i: (0, 0)),
                  row,
                  pl.BlockSpec((1, 1, d), lambda i: (i // bpb, 0, 0)),
                  pl.BlockSpec((d, d), lambda i: (0, 0))],
        out_specs=row,
        out_shape=jax.ShapeDtypeStruct((t, d), F32),
        compiler_params=_params(("arbitrary",)),
        name="sc_out",
    )(z, z, z, conv_w, conv_b.reshape(1, d), x, g1, w16)


def _top_rows(vals, idx_f, fill, n_out, emit):
    for r in range(n_out):
        m = jnp.max(vals, axis=0, keepdims=True)
        am = jnp.min(jnp.where(vals == m, idx_f, fill), axis=0, keepdims=True)
        emit(r, m, am)
        vals = jnp.where(idx_f == am, -jnp.inf, vals)


def _route_kernel(q_ref, k_ref, ids_ref, gate_ref, sv_ref, si_ref, cv_ref, cand_ref, eid_ref, flat_ref):
    nk, dk = k_ref.shape[1], k_ref.shape[2]
    kk = PEER_TOPK
    n_cand = cand_ref.shape[0]
    width = q_ref.shape[0]
    for lo in range(0, q_ref.shape[0], width):
        cols = slice(lo, lo + width)
        key_idx = lax.broadcasted_iota(jnp.int32, (nk, width), 0).astype(F32)
        for p in range(2):
            s_t = lax.dot_general(k_ref[p], q_ref[cols, p * dk:(p + 1) * dk], (((1,), (1,)), ((), ())),
                                  precision=HIGHEST, preferred_element_type=F32)

            def emit(r, m, am, p=p):
                sv_ref[p, r:r + 1, cols] = m
                si_ref[p, r:r + 1, cols] = am

            _top_rows(s_t, key_idx, float(nk), kk, emit)

        sv1, si1 = sv_ref[1, :, cols], si_ref[1, :, cols]
        off = 0
        for i in range(kk):
            n_i = kk // (i + 1)
            cand_ref[off:off + n_i, cols] = sv_ref[0, i:i + 1, cols] + sv1[0:n_i, :]
            eid_ref[off:off + n_i, cols] = si_ref[0, i:i + 1, cols] * float(nk) + si1[0:n_i, :]
            flat_ref[off:off + n_i, cols] = (lax.broadcasted_iota(jnp.int32, (n_i, width), 0).astype(F32)
                                             + float(i * kk))
            off += n_i
        cand_ref[off:n_cand, cols] = jnp.full((n_cand - off, width), -jnp.inf, F32)
        eid_ref[off:n_cand, cols] = jnp.zeros((n_cand - off, width), F32)
        flat_ref[off:n_cand, cols] = jnp.full((n_cand - off, width), float(kk * kk), F32)
        eid, flat = eid_ref[:, cols], flat_ref[:, cols]

        def emit2(r, m, am):
            e = jnp.max(jnp.where(flat == am, eid, -1.0), axis=0, keepdims=True)
            cv_ref[r:r + 1, cols] = m
            ids_ref[r:r + 1, cols] = e.astype(jnp.int32)

        _top_rows(cand_ref[:, cols], flat, float(kk * kk), kk, emit2)
        cv = cv_ref[:, cols]
        ex = jnp.exp(cv - jnp.max(cv, axis=0, keepdims=True))
        gate_ref[:, cols] = ex / jnp.sum(ex, axis=0, keepdims=True)


def _peer_route(q, sub_keys, tb=1024):
    t = q.shape[0]
    nh, _, nk, dk = sub_keys.shape
    kk = PEER_TOPK
    n_cand = sum(kk // (i + 1) for i in range(kk))
    n_cand += (-n_cand) % SUBLANES
    o_spec = pl.BlockSpec((kk, tb), lambda i, h: (h, i))
    return pl.pallas_call(
        _route_kernel,
        grid=(t // tb, nh),
        in_specs=[pl.BlockSpec((tb, 2 * dk), lambda i, h: (i, h)),
                  pl.BlockSpec((None, 2, nk, dk), lambda i, h: (h, 0, 0, 0))],
        out_specs=[o_spec, o_spec],
        out_shape=[jax.ShapeDtypeStruct((nh * kk, t), jnp.int32),
                   jax.ShapeDtypeStruct((nh * kk, t), F32)],
        scratch_shapes=[pltpu.VMEM((2, kk, tb), F32), pltpu.VMEM((2, kk, tb), F32), pltpu.VMEM((kk, tb), F32)]
        + [pltpu.VMEM((n_cand, tb), F32)] * 3,
        compiler_params=_params(("arbitrary", "arbitrary")),
        name="peer_route",
    )(q, sub_keys)


def _pack_pair(x):
    half = x.shape[1] // 2
    bits = pltpu.bitcast(x.astype(BF16).astype(F32), jnp.uint32)
    return bits[:, half:] | (bits[:, :half] >> 16)


def _unpack_pair(w):
    return pltpu.bitcast(w << 16, F32), pltpu.bitcast(w & jnp.uint32(0xFFFF0000), F32)


def _pack_kernel(u_ref, v_ref, o_ref):
    half = u_ref.shape[1] // 2
    o_ref[:, :half] = _pack_pair(u_ref[...])
    o_ref[:, half:] = _pack_pair(v_ref[...])


def _pack_experts(u, v, te=512):
    e, d = u.shape
    blk = pl.BlockSpec((te, d), lambda i: (i, 0))
    return pl.pallas_call(
        _pack_kernel,
        grid=(e // te,),
        in_specs=[blk, blk],
        out_specs=blk,
        out_shape=jax.ShapeDtypeStruct((e, d), jnp.uint32),
        compiler_params=_params(("arbitrary",)),
        name="pack_experts",
    )(u, v)


PEER_TOKENS_PER_STEP = 128
PEER_SLOTS = SUBLANES


def _peer_kernel(final_norm, head_ref, ids_ref, gate_ref, h_ref, x_ref, g2_ref, fin_ref, tab_ref, o_ref,
                 buf_ref, orow_ref, sem_ref):
    tb, d = h_ref.shape
    ne = PEER_PER_TOKEN
    nslots = buf_ref.shape[0]
    nword = d // (2 * LANES)
    ahead = nslots - 1
    step, last_step = pl.program_id(0), pl.num_programs(0) - 1

    def start_rows(src_ids, tok, slot, part=0, nparts=1):
        for k in range(part * ne // nparts, (part + 1) * ne // nparts):
            pltpu.make_async_copy(tab_ref.at[src_ids[tok, k]], buf_ref.at[slot, pl.ds(k, 1), :],
                                  sem_ref.at[slot]).start(priority=k % 2)

    def wait_token(slot):
        pltpu.make_async_copy(tab_ref.at[pl.ds(0, ne), 0], buf_ref.at[slot], sem_ref.at[slot]).wait()

    lane_tok = lax.broadcasted_iota(jnp.int32, (ne, tb), 1)

    def token(base, j, issue_next):
        wait_token(j)
        rows = pl.ds(base, nslots)
        acc = jnp.zeros((ne, LANES), F32)
        for c in range(nword):
            lo, hi = _unpack_pair(buf_ref[j, :, c * LANES:(c + 1) * LANES])
            acc = acc + lo * h_ref[rows, c * LANES:(c + 1) * LANES][j:j + 1, :]
            acc = acc + hi * h_ref[rows, (nword + c) * LANES:(nword + c + 1) * LANES][j:j + 1, :]
            issue_next(c, 2 * nword)
        act = _gelu_tanh(jnp.sum(acc, axis=1, keepdims=True))
        gcol = jnp.sum(jnp.where(lane_tok == base + j, gate_ref[...], 0.0), axis=1, keepdims=True)
        wb = jnp.broadcast_to(gcol * act, (ne, LANES))
        for c in range(nword):
            lo, hi = _unpack_pair(buf_ref[j, :, (nword + c) * LANES:(nword + c + 1) * LANES])
            orow_ref[j:j + 1, c * LANES:(c + 1) * LANES] = jnp.sum(lo * wb, axis=0, keepdims=True)
            orow_ref[j:j + 1, (nword + c) * LANES:(nword + c + 1) * LANES] = jnp.sum(hi * wb, axis=0, keepdims=True)
            issue_next(nword + c, 2 * nword)

    def finish_group(base):
        rows = pl.ds(base, nslots)
        y = x_ref[rows, :] + g2_ref[0] * orow_ref[...]
        if final_norm:
            y = (y * lax.rsqrt(jnp.mean(y * y, axis=-1, keepdims=True) + EPS)) * fin_ref[...]
        o_ref[rows, :] = y

    @pl.when(step == 0)
    def _():
        def first(s, carry):
            start_rows(head_ref, s, s)
            return carry
        lax.fori_loop(0, ahead, first, 0)

    def group(g, carry):
        base = pl.multiple_of(g * nslots, nslots)
        for j in range(nslots):
            token(base, j, functools.partial(start_rows, ids_ref, base + j, (j + ahead) % nslots))
        finish_group(base)
        return carry

    lax.fori_loop(0, tb // nslots, group, 0)

    @pl.when(step == last_step)
    def _():
        for s in range(ahead):
            wait_token(s)


def _peer_experts(ids_t, gates, h, x, g2, table, rows_per_batch, final_gamma, final_norm):
    t, d = x.shape
    tb = PEER_TOKENS_PER_STEP
    ne = PEER_PER_TOKEN
    bpb = rows_per_batch // tb
    ahead = PEER_SLOTS - 1
    ids_ahead = jnp.concatenate([ids_t[ahead:], jnp.zeros((ahead, ne), jnp.int32)], axis=0)
    row = pl.BlockSpec((tb, d), lambda i: (i, 0))
    return pl.pallas_call(
        functools.partial(_peer_kernel, final_norm),
        grid=(t // tb,),
        in_specs=[pl.BlockSpec((PEER_SLOTS, ne), lambda i: (0, 0), memory_space=pltpu.SMEM),
                  pl.BlockSpec((tb, ne), lambda i: (i, 0), memory_space=pltpu.SMEM),
                  pl.BlockSpec((ne, tb), lambda i: (0, i)),
                  row, row,
                  pl.BlockSpec((1, 1, d), lambda i: (i // bpb, 0, 0)),
                  pl.BlockSpec((1, d), lambda i: (0, 0)),
                  pl.BlockSpec(memory_space=pl.ANY)],
        out_specs=row,
        out_shape=jax.ShapeDtypeStruct((t, d), F32),
        scratch_shapes=[pltpu.VMEM((PEER_SLOTS, ne, d), jnp.uint32),
                        pltpu.VMEM((PEER_SLOTS, d), F32),
                        pltpu.SemaphoreType.DMA((PEER_SLOTS,))],
        compiler_params=_params(("arbitrary",)),
        name="peer_experts",
    )(ids_t[:PEER_SLOTS], ids_ahead, gates, h, x, g2, final_gamma.reshape(1, d),
      table.reshape(table.shape[0], 1, d))


def _peer_layer(x, mod, gamma, w_q, sub_keys, u, v, seq, final_gamma, final_norm):
    bsz, d = mod.shape[0], x.shape[1]
    sh2 = mod[:, 3 * d:4 * d].reshape(bsz, 1, d)
    sc2 = mod[:, 4 * d:5 * d].reshape(bsz, 1, d)
    g2 = mod[:, 5 * d:6 * d].reshape(bsz, 1, d)
    q, h = _norm_mod_matmul(x, gamma, sh2, sc2, w_q.astype(BF16), seq, emit_h=True)
    ids, gates = _peer_route(q, sub_keys)
    table = _pack_experts(u, v)
    return _peer_experts(ids.T, gates, h, x, g2, table, seq, final_gamma, final_norm)


def kernel(x, c, ctx, c_ctx, w_mod, b_mod, norm_mix_g, norm_ffn_g, norm_final_g, lru_w_in, lru_conv_w, lru_conv_b, lru_w_a, lru_b_a, lru_w_x, lru_b_x, lru_lambda, lru_w_out, sc_w_in, sc_conv_w, sc_conv_b, sc_w_out, peer_w_q, peer_sub_keys, peer_u, peer_v):
    bsz, seq, d = x.shape
    ctx_len = ctx.shape[1]
    xt = x.reshape(bsz * seq, d)

    c_rows = jnp.concatenate([c, c_ctx[None, :], jnp.zeros((SUBLANES - bsz - 1, d), F32)], axis=0)
    mods = [_mod_vectors(c_rows, w_mod[i], b_mod[i]) for i in range(w_mod.shape[0])]

    mod = mods[0][:bsz]
    sh1, sc1, g1 = (mod[:, k * d:(k + 1) * d].reshape(bsz, 1, d) for k in range(3))
    w_in16 = lru_w_in[0].astype(BF16)
    w_a16, w_x16 = lru_w_a[0].astype(BF16), lru_w_x[0].astype(BF16)
    z = _norm_mod_matmul(xt, norm_mix_g[0], sh1, sc1, w_in16, seq, out_dtype=BF16)

    mod_c = mods[0][bsz:bsz + 1]
    sh_c = jnp.broadcast_to(mod_c[:, 0:d].reshape(1, 1, d), (bsz, 1, d))
    sc_c = jnp.broadcast_to(mod_c[:, d:2 * d].reshape(1, 1, d), (bsz, 1, d))
    z_ctx = _norm_mod_matmul(ctx.reshape(bsz * ctx_len, d), norm_mix_g[0], sh_c, sc_c, w_in16, ctx_len,
                             out_dtype=BF16)
    coef_args = (1, lru_conv_w[0], lru_conv_b[0], w_a16, lru_b_a[0], w_x16, lru_b_x[0], lru_lambda[0])
    cf = [t.reshape(bsz, ctx_len, d) for t in _lru_coeffs(z_ctx, *coef_args, ctx_len, ctx_len)]
    zero_state = jnp.zeros((bsz, 1, d), F32)
    hc_f = _linear_scan(cf[0], cf[1], zero_state, False)
    hc_r = _linear_scan(cf[2], cf[3], zero_state, True)

    lf = [t.reshape(bsz, seq, d) for t in _lru_coeffs(z, *coef_args, GRID_W, 256)]
    h_f = _linear_scan(lf[0], lf[1], hc_f[:, ctx_len - 1:ctx_len, :], False)
    y = _linear_scan(lf[2], lf[3], hc_r[:, 0:1, :], True, prev=h_f)
    xt = _lru_out(z, y.reshape(bsz * seq, d), xt, g1, lru_w_out[0].astype(BF16), seq)
    xt = _peer_layer(xt, mod, norm_ffn_g[0], peer_w_q[0], peer_sub_keys[0], peer_u[0], peer_v[0], seq,
                     norm_final_g, False)

    mod = mods[1][:bsz]
    sh1, sc1, g1 = (mod[:, k * d:(k + 1) * d].reshape(bsz, 1, d) for k in range(3))
    z = _norm_mod_matmul(xt, norm_mix_g[1], sh1, sc1, sc_w_in[0].astype(BF16), seq, out_dtype=BF16)
    xt = _sc_out(z, sc_conv_w[0], sc_conv_b[0], xt, g1, sc_w_out[0].astype(BF16), seq)
    xt = _peer_layer(xt, mod, norm_ffn_g[1], peer_w_q[1], peer_sub_keys[1], peer_u[1], peer_v[1], seq,
                     norm_final_g, True)
    return xt.reshape(bsz, seq, d)
```

```python
import functools

import jax
import jax.numpy as jnp
from jax import lax
from jax.experimental import pallas as pl
from jax.experimental.pallas import tpu as pltpu

EPS = 1e-6
GRID_W = 64
LRU_HEADS = 16
LRU_C = 8.0
PEER_HEADS = 8
PEER_N_KEYS = 128
PEER_TOPK = 16
PEER_PER_TOKEN = PEER_HEADS * PEER_TOPK

LANES = 128
SUBLANES = 8
VMEM_LIMIT = 48 * 1024 * 1024

F32 = jnp.float32
BF16 = jnp.bfloat16
HIGHEST = lax.Precision.HIGHEST


def _params(sem):
    return pltpu.CompilerParams(dimension_semantics=sem, vmem_limit_bytes=VMEM_LIMIT)


def _gelu_tanh(x):
    c = 0.7978845608028654
    return x * (0.5 * (1.0 + jnp.tanh(c * (x + 0.044715 * (x * x * x)))))


def _neg_expm1_2x(half_z, exp_z):
    return -jnp.tanh(half_z) * (exp_z + 1.0)


def _sigmoid(x):
    return 0.5 * (jnp.tanh(0.5 * x) + 1.0)


def _mod_kernel(c_ref, w_ref, b_ref, o_ref):
    c = c_ref[...]
    s = c * jax.nn.sigmoid(c)
    o_ref[...] = jnp.dot(s, w_ref[...], preferred_element_type=F32, precision=HIGHEST) + b_ref[...]


def _mod_vectors(c_rows, w_all, b_all, layer):
    nl, d, n = w_all.shape
    tn = 1024
    return pl.pallas_call(
        _mod_kernel,
        grid=(n // tn,),
        in_specs=[pl.BlockSpec((SUBLANES, d), lambda j: (0, 0)),
                  pl.BlockSpec((None, d, tn), lambda j: (layer, 0, j)),
                  pl.BlockSpec((None, 1, tn), lambda j: (layer, 0, j))],
        out_specs=pl.BlockSpec((SUBLANES, tn), lambda j: (0, j)),
        out_shape=jax.ShapeDtypeStruct((SUBLANES, n), F32),
        compiler_params=_params(("arbitrary",)),
        name="mod_vectors",
    )(c_rows, w_all, b_all.reshape(nl, 1, n))


def _nmm_kernel(emit_h, x_ref, g_ref, sh_ref, sc_ref, w_ref, o_ref, *rest):
    if emit_h:
        h_ref, hn_ref = rest
    else:
        (hn_ref,) = rest

    @pl.when(pl.program_id(1) == 0)
    def _():
        x = x_ref[...]
        ms = jnp.mean(x * x, axis=-1, keepdims=True)
        y = (x * lax.rsqrt(ms + EPS)) * g_ref[...]
        h = y * (1.0 + sc_ref[0]) + sh_ref[0]
        hn_ref[...] = h.astype(BF16)
        if emit_h:
            h_ref[...] = h

    o_ref[...] = jnp.dot(hn_ref[...], w_ref[...], preferred_element_type=F32).astype(o_ref.dtype)


def _norm_mod_matmul(x, gamma, shift, scale, w_bf16, rows_per_batch, emit_h=False, out_dtype=F32):
    t, d = x.shape
    n = w_bf16.shape[1]
    tn = 1024
    tm = min(512 if emit_h else 1024, rows_per_batch)
    bpb = rows_per_batch // tm
    out_shape = [jax.ShapeDtypeStruct((t, n), out_dtype)]
    out_specs = [pl.BlockSpec((tm, tn), lambda i, j: (i, j))]
    if emit_h:
        out_shape.append(jax.ShapeDtypeStruct((t, d), F32))
        out_specs.append(pl.BlockSpec((tm, d), lambda i, j: (i, 0)))
    res = pl.pallas_call(
        functools.partial(_nmm_kernel, emit_h),
        grid=(t // tm, n // tn),
        in_specs=[pl.BlockSpec((tm, d), lambda i, j: (i, 0)),
                  pl.BlockSpec((1, d), lambda i, j: (0, 0)),
                  pl.BlockSpec((1, 1, d), lambda i, j: (i // bpb, 0, 0)),
                  pl.BlockSpec((1, 1, d), lambda i, j: (i // bpb, 0, 0)),
                  pl.BlockSpec((d, tn), lambda i, j: (0, j))],
        out_specs=out_specs,
        out_shape=out_shape,
        scratch_shapes=[pltpu.VMEM((tm, d), BF16)],
        compiler_params=_params(("arbitrary", "arbitrary")),
        name="norm_mod_matmul",
    )(x, gamma.reshape(1, d), shift, scale, w_bf16)
    return res if emit_h else res[0]


def _row_conv(u, taps, bias, pad_left, row_len):
    tm = u.shape[0]
    pos = lax.broadcasted_iota(jnp.int32, (tm, 1), 0) % row_len
    y = None
    for k in range(len(taps)):
        off = k - pad_left
        if off == 0:
            term = u
        else:
            rolled = pltpu.roll(u, (-off) % tm, axis=0)
            valid = jnp.logical_and(pos + off >= 0, pos + off < row_len)
            term = jnp.where(valid, rolled, 0.0)
        y = (bias + taps[k] * term) if y is None else (y + taps[k] * term)
    return y


def _lru_coef_kernel(row_len, xb_ref, cw_ref, cb_ref, wa_ref, ba_ref, wx_ref, bx_ref, lam_ref,
                     af_ref, bf_ref, ar_ref, br_ref):
    taps = [cw_ref[k:k + 1, :] for k in range(cw_ref.shape[0])]
    xc = _row_conv(xb_ref[...].astype(F32), taps, cb_ref[...], 2, row_len)
    d = xc.shape[1]
    hd = d // LRU_HEADS
    outs = ((af_ref, bf_ref), (ar_ref, br_ref))
    for dr in range(2):
        lam = lam_ref[dr:dr + 1, :]
        z = -lam
        nsp = (-LRU_C) * (jnp.maximum(z, 0.0) + jnp.log1p(jnp.exp(-jnp.abs(z))))
        a_ref, b_ref = outs[dr]
        for h in range(LRU_HEADS):
            sl = slice(h * hd, (h + 1) * hd)
            xh = xc[:, sl]
            xh16 = xh.astype(BF16)
            r = _sigmoid(jnp.dot(xh16, wa_ref[dr, h], preferred_element_type=F32) + ba_ref[dr:dr + 1, sl])
            i = _sigmoid(jnp.dot(xh16, wx_ref[dr, h], preferred_element_type=F32) + bx_ref[dr:dr + 1, sl])
            log_a = r * nsp[:, sl]
            a = jnp.exp(log_a)
            a_ref[:, sl] = a
            b_ref[:, sl] = jnp.sqrt(_neg_expm1_2x(log_a, a * a)) * (i * xh)


def _lru_coeffs(z, col_block, conv_w, conv_b, w_a16, b_a, w_x16, b_x, lam, row_len, tm):
    t = z.shape[0]
    d = conv_w.shape[1]
    full = lambda *s: pl.BlockSpec(s, lambda i: (0,) * len(s))
    o_spec = pl.BlockSpec((tm, d), lambda i: (i, 0))
    o_shape = jax.ShapeDtypeStruct((t, d), F32)
    return pl.pallas_call(
        functools.partial(_lru_coef_kernel, row_len),
        grid=(t // tm,),
        in_specs=[pl.BlockSpec((tm, d), lambda i: (i, col_block)),
                  full(*conv_w.shape), full(1, d),
                  full(*w_a16.shape), full(2, d), full(*w_x16.shape), full(2, d), full(2, d)],
        out_specs=[o_spec] * 4,
        out_shape=[o_shape] * 4,
        compiler_params=_params(("arbitrary",)),
        name="lru_coeffs",
    )(z, conv_w, conv_b.reshape(1, d), w_a16, b_a, w_x16, b_x, lam)


def _scan_kernel(reverse, add_prev, a_ref, b_ref, h0_ref, *rest):
    if add_prev:
        y_ref, o_ref, st_ref = rest
    else:
        o_ref, st_ref = rest
    ts, cw = a_ref.shape
    ng = ts // SUBLANES

    @pl.when(pl.program_id(2) == 0)
    def _():
        st_ref[...] = jnp.broadcast_to(h0_ref[...], st_ref.shape)

    row = lax.broadcasted_iota(jnp.int32, (SUBLANES, cw), 0)

    def body(gi, h):
        g = (ng - 1 - gi) if reverse else gi
        off = pl.multiple_of(g * SUBLANES, SUBLANES)
        a = a_ref[pl.ds(off, SUBLANES), :]
        b = b_ref[pl.ds(off, SUBLANES), :]
        for s in (1, 2, 4):
            if reverse:
                shift, m = SUBLANES - s, row < SUBLANES - s
            else:
                shift, m = s, row >= s
            a_s = jnp.where(m, pltpu.roll(a, shift, axis=0), 1.0)
            b_s = jnp.where(m, pltpu.roll(b, shift, axis=0), 0.0)
            b = a * b_s + b
            a = a * a_s
        hh = a * h + b
        if add_prev:
            o_ref[pl.ds(off, SUBLANES), :] = hh + y_ref[pl.ds(off, SUBLANES), :]
        else:
            o_ref[pl.ds(off, SUBLANES), :] = hh
        last = hh[0:1, :] if reverse else hh[SUBLANES - 1:SUBLANES, :]
        return jnp.broadcast_to(last, (SUBLANES, cw))

    st_ref[...] = lax.fori_loop(0, ng, body, st_ref[...], unroll=2)


def _linear_scan(a, b, h0, reverse, prev=None, ts=1024, cw=512):
    bsz, s, c = a.shape
    ts = min(ts, s)
    ns = s // ts
    tmap = (lambda bb, cc, j: (bb, ns - 1 - j, cc)) if reverse else (lambda bb, cc, j: (bb, j, cc))
    blk = pl.BlockSpec((None, ts, cw), tmap)
    ins = [a, b, h0]
    in_specs = [blk, blk, pl.BlockSpec((None, 1, cw), lambda bb, cc, j: (bb, 0, cc))]
    if prev is not None:
        ins.append(prev)
        in_specs.append(blk)
    return pl.pallas_call(
        functools.partial(_scan_kernel, reverse, prev is not None),
        grid=(bsz, c // cw, ns),
        in_specs=in_specs,
        out_specs=blk,
        out_shape=jax.ShapeDtypeStruct((bsz, s, c), F32),
        scratch_shapes=[pltpu.VMEM((SUBLANES, cw), F32)],
        compiler_params=_params(("arbitrary", "arbitrary", "arbitrary")),
        name="linear_scan",
    )(*ins)


def _lru_out_kernel(gate_ref, y_ref, x_ref, g1_ref, w_ref, o_ref):
    p = _gelu_tanh(gate_ref[...].astype(F32)) * y_ref[...]
    out = jnp.dot(p.astype(BF16), w_ref[...], preferred_element_type=F32)
    o_ref[...] = x_ref[...] + g1_ref[0] * out


def _lru_out(z, y, x, g1, w16, rows_per_batch, tm=512):
    t, d = x.shape
    bpb = rows_per_batch // tm
    row = pl.BlockSpec((tm, d), lambda i: (i, 0))
    return pl.pallas_call(
        _lru_out_kernel,
        grid=(t // tm,),
        in_specs=[row, row, row,
                  pl.BlockSpec((1, 1, d), lambda i: (i // bpb, 0, 0)),
                  pl.BlockSpec((d, d), lambda i: (0, 0), pipeline_mode=pl.Buffered(1))],
        out_specs=row,
        out_shape=jax.ShapeDtypeStruct((t, d), F32),
        compiler_params=_params(("arbitrary",)),
        name="lru_out",
    )(z, y, x, g1, w16)


def _sc_out_kernel(bg_ref, cg_ref, v_ref, cw_ref, cb_ref, x_ref, g1_ref, w_ref, o_ref):
    taps = [cw_ref[k:k + 1, :] for k in range(cw_ref.shape[0])]
    y = _row_conv(cg_ref[...].astype(F32) * v_ref[...].astype(F32), taps, cb_ref[...], 1, GRID_W)
    p = bg_ref[...].astype(F32) * y
    out = jnp.dot(p.astype(BF16), w_ref[...], preferred_element_type=F32)
    o_ref[...] = x_ref[...] + g1_ref[0] * out


def _sc_out(z, conv_w, conv_b, x, g1, w16, rows_per_batch, tm=512):
    t, d = x.shape
    bpb = rows_per_batch // tm
    row = pl.BlockSpec((tm, d), lambda i: (i, 0))
    col = lambda c: pl.BlockSpec((tm, d), lambda i: (i, c))
    return pl.pallas_call(
        _sc_out_kernel,
        grid=(t // tm,),
        in_specs=[col(0), col(1), col(2),
                  pl.BlockSpec(conv_w.shape, lambda i: (0, 0)),
                  pl.BlockSpec((1, d), lambda i: (0, 0)),
                  row,
                  pl.BlockSpec((1, 1, d), lambda i: (i // bpb, 0, 0)),
                  pl.BlockSpec((d, d), lambda i: (0, 0), pipeline_mode=pl.Buffered(1))],
        out_specs=row,
        out_shape=jax.ShapeDtypeStruct((t, d), F32),
        compiler_params=_params(("arbitrary",)),
        name="sc_out",
    )(z, z, z, conv_w, conv_b.reshape(1, d), x, g1, w16)


def _top_rows(vals, idx_f, fill, n_out, emit):
    for r in range(n_out):
        m = jnp.max(vals, axis=0, keepdims=True)
        am = jnp.min(jnp.where(vals == m, idx_f, fill), axis=0, keepdims=True)
        emit(r, m, am)
        vals = jnp.where(idx_f == am, -jnp.inf, vals)


def _route_kernel(q_ref, k_ref, ids_ref, gate_ref, sv_ref, si_ref, cv_ref, cand_ref, eid_ref, flat_ref):
    nk, dk = k_ref.shape[1], k_ref.shape[2]
    kk = PEER_TOPK
    n_cand = cand_ref.shape[0]
    width = q_ref.shape[0]
    for lo in range(0, q_ref.shape[0], width):
        cols = slice(lo, lo + width)
        key_idx = lax.broadcasted_iota(jnp.int32, (nk, width), 0).astype(F32)
        for p in range(2):
            s_t = lax.dot_general(k_ref[p], q_ref[cols, p * dk:(p + 1) * dk], (((1,), (1,)), ((), ())),
                                  precision=HIGHEST, preferred_element_type=F32)

            def emit(r, m, am, p=p):
                sv_ref[p, r:r + 1, cols] = m
                si_ref[p, r:r + 1, cols] = am

            _top_rows(s_t, key_idx, float(nk), kk, emit)

        sv1, si1 = sv_ref[1, :, cols], si_ref[1, :, cols]
        off = 0
        for i in range(kk):
            n_i = kk // (i + 1)
            cand_ref[off:off + n_i, cols] = sv_ref[0, i:i + 1, cols] + sv1[0:n_i, :]
            eid_ref[off:off + n_i, cols] = si_ref[0, i:i + 1, cols] * float(nk) + si1[0:n_i, :]
            flat_ref[off:off + n_i, cols] = (lax.broadcasted_iota(jnp.int32, (n_i, width), 0).astype(F32)
                                             + float(i * kk))
            off += n_i
        cand_ref[off:n_cand, cols] = jnp.full((n_cand - off, width), -jnp.inf, F32)
        eid_ref[off:n_cand, cols] = jnp.zeros((n_cand - off, width), F32)
        flat_ref[off:n_cand, cols] = jnp.full((n_cand - off, width), float(kk * kk), F32)
        eid, flat = eid_ref[:, cols], flat_ref[:, cols]

        def emit2(r, m, am):
            e = jnp.max(jnp.where(flat == am, eid, -1.0), axis=0, keepdims=True)
            cv_ref[r:r + 1, cols] = m
            ids_ref[r:r + 1, cols] = e.astype(jnp.int32)

        _top_rows(cand_ref[:, cols], flat, float(kk * kk), kk, emit2)
        cv = cv_ref[:, cols]
        ex = jnp.exp(cv - jnp.max(cv, axis=0, keepdims=True))
        gate_ref[:, cols] = ex / jnp.sum(ex, axis=0, keepdims=True)


def _peer_route(q, sub_keys, tb=1024):
    t = q.shape[0]
    nh, _, nk, dk = sub_keys.shape
    kk = PEER_TOPK
    n_cand = sum(kk // (i + 1) for i in range(kk))
    n_cand += (-n_cand) % SUBLANES
    o_spec = pl.BlockSpec((kk, tb), lambda i, h: (h, i))
    return pl.pallas_call(
        _route_kernel,
        grid=(t // tb, nh),
        in_specs=[pl.BlockSpec((tb, 2 * dk), lambda i, h: (i, h)),
                  pl.BlockSpec((None, 2, nk, dk), lambda i, h: (h, 0, 0, 0))],
        out_specs=[o_spec, o_spec],
        out_shape=[jax.ShapeDtypeStruct((nh * kk, t), jnp.int32),
                   jax.ShapeDtypeStruct((nh * kk, t), F32)],
        scratch_shapes=[pltpu.VMEM((2, kk, tb), F32), pltpu.VMEM((2, kk, tb), F32), pltpu.VMEM((kk, tb), F32)]
        + [pltpu.VMEM((n_cand, tb), F32)] * 3,
        compiler_params=_params(("arbitrary", "arbitrary")),
        name="peer_route",
    )(q, sub_keys)


def _pack_pair(x):
    half = x.shape[1] // 2
    bits = pltpu.bitcast(x.astype(BF16).astype(F32), jnp.uint32)
    return bits[:, half:] | (bits[:, :half] >> 16)


def _unpack_pair(w):
    return pltpu.bitcast(w << 16, F32), pltpu.bitcast(w & jnp.uint32(0xFFFF0000), F32)


def _pack_kernel(u_ref, v_ref, o_ref):
    half = u_ref.shape[1] // 2
    o_ref[:, :half] = _pack_pair(u_ref[...])
    o_ref[:, half:] = _pack_pair(v_ref[...])


def _pack_experts(u_all, v_all, layer, te=512):
    _, e, d = u_all.shape
    src = pl.BlockSpec((None, te, d), lambda i: (layer, i, 0))
    return pl.pallas_call(
        _pack_kernel,
        grid=(e // te,),
        in_specs=[src, src],
        out_specs=pl.BlockSpec((te, d), lambda i: (i, 0)),
        out_shape=jax.ShapeDtypeStruct((e, d), jnp.uint32),
        compiler_params=_params(("arbitrary",)),
        name="pack_experts",
    )(u_all, v_all)


PEER_TOKENS_PER_STEP = 128
PEER_SLOTS = SUBLANES


def _peer_kernel(final_norm, head_ref, ids_ref, gate_ref, h_ref, x_ref, g2_ref, fin_ref, tab_ref, o_ref,
                 buf_ref, orow_ref, sem_ref):
    tb, d = h_ref.shape
    ne = PEER_PER_TOKEN
    nslots = buf_ref.shape[0]
    nword = d // (2 * LANES)
    ahead = nslots - 1
    step, last_step = pl.program_id(0), pl.num_programs(0) - 1

    def start_rows(src_ids, tok, slot, part=0, nparts=1):
        for k in range(part * ne // nparts, (part + 1) * ne // nparts):
            pltpu.make_async_copy(tab_ref.at[src_ids[tok, k]], buf_ref.at[slot, pl.ds(k, 1), :],
                                  sem_ref.at[slot]).start(priority=k % 2)

    def wait_token(slot):
        pltpu.make_async_copy(tab_ref.at[pl.ds(0, ne), 0], buf_ref.at[slot], sem_ref.at[slot]).wait()

    lane_tok = lax.broadcasted_iota(jnp.int32, (ne, tb), 1)

    def token(base, j, issue_next):
        wait_token(j)
        rows = pl.ds(base, nslots)
        acc = jnp.zeros((ne, LANES), F32)
        for c in range(nword):
            lo, hi = _unpack_pair(buf_ref[j, :, c * LANES:(c + 1) * LANES])
            acc = acc + lo * h_ref[rows, c * LANES:(c + 1) * LANES][j:j + 1, :]
            acc = acc + hi * h_ref[rows, (nword + c) * LANES:(nword + c + 1) * LANES][j:j + 1, :]
            issue_next(c, 2 * nword)
        act = _gelu_tanh(jnp.sum(acc, axis=1, keepdims=True))
        gcol = jnp.sum(jnp.where(lane_tok == base + j, gate_ref[...], 0.0), axis=1, keepdims=True)
        wb = jnp.broadcast_to(gcol * act, (ne, LANES))
        for c in range(nword):
            lo, hi = _unpack_pair(buf_ref[j, :, (nword + c) * LANES:(nword + c + 1) * LANES])
            orow_ref[j:j + 1, c * LANES:(c + 1) * LANES] = jnp.sum(lo * wb, axis=0, keepdims=True)
            orow_ref[j:j + 1, (nword + c) * LANES:(nword + c + 1) * LANES] = jnp.sum(hi * wb, axis=0, keepdims=True)
            issue_next(nword + c, 2 * nword)

    def finish_group(base):
        rows = pl.ds(base, nslots)
        y = x_ref[rows, :] + g2_ref[0] * orow_ref[...]
        if final_norm:
            y = (y * lax.rsqrt(jnp.mean(y * y, axis=-1, keepdims=True) + EPS)) * fin_ref[...]
        o_ref[rows, :] = y

    @pl.when(step == 0)
    def _():
        def first(s, carry):
            start_rows(head_ref, s, s)
            return carry
        lax.fori_loop(0, ahead, first, 0)

    def group(g, carry):
        base = pl.multiple_of(g * nslots, nslots)
        for j in range(nslots):
            token(base, j, functools.partial(start_rows, ids_ref, base + j, (j + ahead) % nslots))
        finish_group(base)
        return carry

    lax.fori_loop(0, tb // nslots, group, 0)

    @pl.when(step == last_step)
    def _():
        for s in range(ahead):
            wait_token(s)


def _peer_experts(ids_t, gates, h, x, g2, table, rows_per_batch, final_gamma, final_norm):
    t, d = x.shape
    tb = PEER_TOKENS_PER_STEP
    ne = PEER_PER_TOKEN
    bpb = rows_per_batch // tb
    ahead = PEER_SLOTS - 1
    ids_ahead = jnp.concatenate([ids_t[ahead:], jnp.zeros((ahead, ne), jnp.int32)], axis=0)
    row = pl.BlockSpec((tb, d), lambda i: (i, 0))
    return pl.pallas_call(
        functools.partial(_peer_kernel, final_norm),
        grid=(t // tb,),
        in_specs=[pl.BlockSpec((PEER_SLOTS, ne), lambda i: (0, 0), memory_space=pltpu.SMEM),
                  pl.BlockSpec((tb, ne), lambda i: (i, 0), memory_space=pltpu.SMEM),
                  pl.BlockSpec((ne, tb), lambda i: (0, i)),
                  row, row,
                  pl.BlockSpec((1, 1, d), lambda i: (i // bpb, 0, 0)),
                  pl.BlockSpec((1, d), lambda i: (0, 0)),
                  pl.BlockSpec(memory_space=pl.ANY)],
        out_specs=row,
        out_shape=jax.ShapeDtypeStruct((t, d), F32),
        scratch_shapes=[pltpu.VMEM((PEER_SLOTS, ne, d), jnp.uint32),
                        pltpu.VMEM((PEER_SLOTS, d), F32),
                        pltpu.SemaphoreType.DMA((PEER_SLOTS,))],
        compiler_params=_params(("arbitrary",)),
        name="peer_experts",
    )(ids_t[:PEER_SLOTS], ids_ahead, gates, h, x, g2, final_gamma.reshape(1, d),
      table.reshape(table.shape[0], 1, d))


def _peer_layer(x, mod, gamma, w_q, sub_keys, u_all, v_all, layer, seq, final_gamma, final_norm):
    bsz, d = mod.shape[0], x.shape[1]
    sh2 = mod[:, 3 * d:4 * d].reshape(bsz, 1, d)
    sc2 = mod[:, 4 * d:5 * d].reshape(bsz, 1, d)
    g2 = mod[:, 5 * d:6 * d].reshape(bsz, 1, d)
    q, h = _norm_mod_matmul(x, gamma, sh2, sc2, w_q.astype(BF16), seq, emit_h=True)
    ids, gates = _peer_route(q, sub_keys)
    table = _pack_experts(u_all, v_all, layer)
    return _peer_experts(ids.T, gates, h, x, g2, table, seq, final_gamma, final_norm)


def kernel(x, c, ctx, c_ctx, w_mod, b_mod, norm_mix_g, norm_ffn_g, norm_final_g, lru_w_in, lru_conv_w, lru_conv_b, lru_w_a, lru_b_a, lru_w_x, lru_b_x, lru_lambda, lru_w_out, sc_w_in, sc_conv_w, sc_conv_b, sc_w_out, peer_w_q, peer_sub_keys, peer_u, peer_v):
    bsz, seq, d = x.shape
    ctx_len = ctx.shape[1]
    xt = x.reshape(bsz * seq, d)

    c_rows = jnp.concatenate([c, c_ctx[None, :], jnp.zeros((SUBLANES - bsz - 1, d), F32)], axis=0)
    mods = [_mod_vectors(c_rows, w_mod, b_mod, i) for i in range(w_mod.shape[0])]

    mod = mods[0][:bsz]
    sh1, sc1, g1 = (mod[:, k * d:(k + 1) * d].reshape(bsz, 1, d) for k in range(3))
    w_in16 = lru_w_in[0].astype(BF16)
    w_a16, w_x16 = lru_w_a[0].astype(BF16), lru_w_x[0].astype(BF16)
    z = _norm_mod_matmul(xt, norm_mix_g[0], sh1, sc1, w_in16, seq, out_dtype=BF16)

    mod_c = mods[0][bsz:bsz + 1]
    sh_c = jnp.broadcast_to(mod_c[:, 0:d].reshape(1, 1, d), (bsz, 1, d))
    sc_c = jnp.broadcast_to(mod_c[:, d:2 * d].reshape(1, 1, d), (bsz, 1, d))
    z_ctx = _norm_mod_matmul(ctx.reshape(bsz * ctx_len, d), norm_mix_g[0], sh_c, sc_c, w_in16, ctx_len,
                             out_dtype=BF16)
    coef_args = (1, lru_conv_w[0], lru_conv_b[0], w_a16, lru_b_a[0], w_x16, lru_b_x[0], lru_lambda[0])
    cf = [t.reshape(bsz, ctx_len, d) for t in _lru_coeffs(z_ctx, *coef_args, ctx_len, ctx_len)]
    zero_state = jnp.zeros((bsz, 1, d), F32)
    hc_f = _linear_scan(cf[0], cf[1], zero_state, False)
    hc_r = _linear_scan(cf[2], cf[3], zero_state, True)

    lf = [t.reshape(bsz, seq, d) for t in _lru_coeffs(z, *coef_args, GRID_W, 256)]
    h_f = _linear_scan(lf[0], lf[1], hc_f[:, ctx_len - 1:ctx_len, :], False)
    y = _linear_scan(lf[2], lf[3], hc_r[:, 0:1, :], True, prev=h_f)
    xt = _lru_out(z, y.reshape(bsz * seq, d), xt, g1, lru_w_out[0].astype(BF16), seq)
    xt = _peer_layer(xt, mod, norm_ffn_g[0], peer_w_q[0], peer_sub_keys[0], peer_u, peer_v, 0, seq,
                     norm_final_g, False)

    mod = mods[1][:bsz]
    sh1, sc1, g1 = (mod[:, k * d:(k + 1) * d].reshape(bsz, 1, d) for k in range(3))
    z = _norm_mod_matmul(xt, norm_mix_g[1], sh1, sc1, sc_w_in[0].astype(BF16), seq, out_dtype=BF16)
    xt = _sc_out(z, sc_conv_w[0], sc_conv_b[0], xt, g1, sc_w_out[0].astype(BF16), seq)
    xt = _peer_layer(xt, mod, norm_ffn_g[1], peer_w_q[1], peer_sub_keys[1], peer_u, peer_v, 1, seq,
                     norm_final_g, True)
    return xt.reshape(bsz, seq, d)
```

```python
import functools

import jax
import jax.numpy as jnp
from jax import lax
from jax.experimental import pallas as pl
from jax.experimental.pallas import tpu as pltpu

EPS = 1e-6
GRID_W = 64
LRU_HEADS = 16
LRU_C = 8.0
PEER_HEADS = 8
PEER_N_KEYS = 128
PEER_TOPK = 16
PEER_PER_TOKEN = PEER_HEADS * PEER_TOPK

LANES = 128
SUBLANES = 8
VMEM_LIMIT = 48 * 1024 * 1024

F32 = jnp.float32
BF16 = jnp.bfloat16
HIGHEST = lax.Precision.HIGHEST


def _params(sem):
    return pltpu.CompilerParams(dimension_semantics=sem, vmem_limit_bytes=VMEM_LIMIT)


def _gelu_tanh(x):
    c = 0.7978845608028654
    return x * (0.5 * (1.0 + jnp.tanh(c * (x + 0.044715 * (x * x * x)))))


def _neg_expm1_2x(half_z, exp_z):
    return -jnp.tanh(half_z) * (exp_z + 1.0)


def _sigmoid(x):
    return 0.5 * (jnp.tanh(0.5 * x) + 1.0)


def _mod_kernel(c_ref, w_ref, b_ref, o_ref):
    c = c_ref[...]
    s = c * jax.nn.sigmoid(c)
    o_ref[...] = jnp.dot(s, w_ref[...], preferred_element_type=F32, precision=HIGHEST) + b_ref[...]


def _mod_vectors(c_rows, w_all, b_all, layer):
    nl, d, n = w_all.shape
    tn = 1024
    return pl.pallas_call(
        _mod_kernel,
        grid=(n // tn,),
        in_specs=[pl.BlockSpec((SUBLANES, d), lambda j: (0, 0)),
                  pl.BlockSpec((None, d, tn), lambda j: (layer, 0, j)),
                  pl.BlockSpec((None, 1, tn), lambda j: (layer, 0, j))],
        out_specs=pl.BlockSpec((SUBLANES, tn), lambda j: (0, j)),
        out_shape=jax.ShapeDtypeStruct((SUBLANES, n), F32),
        compiler_params=_params(("arbitrary",)),
        name="mod_vectors",
    )(c_rows, w_all, b_all.reshape(nl, 1, n))


def _nmm_kernel(emit_h, x_ref, g_ref, sh_ref, sc_ref, w_ref, o_ref, *rest):
    if emit_h:
        h_ref, hn_ref = rest
    else:
        (hn_ref,) = rest

    @pl.when(pl.program_id(1) == 0)
    def _():
        x = x_ref[...]
        ms = jnp.mean(x * x, axis=-1, keepdims=True)
        y = (x * lax.rsqrt(ms + EPS)) * g_ref[...]
        h = y * (1.0 + sc_ref[0]) + sh_ref[0]
        hn_ref[...] = h.astype(BF16)
        if emit_h:
            h_ref[...] = h

    o_ref[...] = jnp.dot(hn_ref[...], w_ref[...], preferred_element_type=F32).astype(o_ref.dtype)


def _norm_mod_matmul(x, gamma, shift, scale, w_bf16, rows_per_batch, emit_h=False, out_dtype=F32):
    t, d = x.shape
    n = w_bf16.shape[1]
    tn = 1024
    tm = min(512 if emit_h else 1024, rows_per_batch)
    bpb = rows_per_batch // tm
    out_shape = [jax.ShapeDtypeStruct((t, n), out_dtype)]
    out_specs = [pl.BlockSpec((tm, tn), lambda i, j: (i, j))]
    if emit_h:
        out_shape.append(jax.ShapeDtypeStruct((t, d), F32))
        out_specs.append(pl.BlockSpec((tm, d), lambda i, j: (i, 0)))
    res = pl.pallas_call(
        functools.partial(_nmm_kernel, emit_h),
        grid=(t // tm, n // tn),
        in_specs=[pl.BlockSpec((tm, d), lambda i, j: (i, 0)),
                  pl.BlockSpec((1, d), lambda i, j: (0, 0)),
                  pl.BlockSpec((1, 1, d), lambda i, j: (i // bpb, 0, 0)),
                  pl.BlockSpec((1, 1, d), lambda i, j: (i // bpb, 0, 0)),
                  pl.BlockSpec((d, tn), lambda i, j: (0, j))],
        out_specs=out_specs,
        out_shape=out_shape,
        scratch_shapes=[pltpu.VMEM((tm, d), BF16)],
        compiler_params=_params(("arbitrary", "arbitrary")),
        name="norm_mod_matmul",
    )(x, gamma.reshape(1, d), shift, scale, w_bf16)
    return res if emit_h else res[0]


def _row_conv(u, taps, bias, pad_left, row_len):
    tm = u.shape[0]
    pos = lax.broadcasted_iota(jnp.int32, (tm, 1), 0) % row_len
    y = None
    for k in range(len(taps)):
        off = k - pad_left
        if off == 0:
            term = u
        else:
            rolled = pltpu.roll(u, (-off) % tm, axis=0)
            valid = jnp.logical_and(pos + off >= 0, pos + off < row_len)
            term = jnp.where(valid, rolled, 0.0)
        y = (bias + taps[k] * term) if y is None else (y + taps[k] * term)
    return y


def _lru_coef_kernel(row_len, xb_ref, cw_ref, cb_ref, wa_ref, ba_ref, wx_ref, bx_ref, lam_ref,
                     af_ref, bf_ref, ar_ref, br_ref):
    taps = [cw_ref[k:k + 1, :] for k in range(cw_ref.shape[0])]
    xc = _row_conv(xb_ref[...].astype(F32), taps, cb_ref[...], 2, row_len)
    d = xc.shape[1]
    hd = d // LRU_HEADS
    outs = ((af_ref, bf_ref), (ar_ref, br_ref))
    for dr in range(2):
        lam = lam_ref[dr:dr + 1, :]
        z = -lam
        nsp = (-LRU_C) * (jnp.maximum(z, 0.0) + jnp.log1p(jnp.exp(-jnp.abs(z))))
        a_ref, b_ref = outs[dr]
        for h in range(LRU_HEADS):
            sl = slice(h * hd, (h + 1) * hd)
            xh = xc[:, sl]
            xh16 = xh.astype(BF16)
            r = _sigmoid(jnp.dot(xh16, wa_ref[dr, h], preferred_element_type=F32) + ba_ref[dr:dr + 1, sl])
            i = _sigmoid(jnp.dot(xh16, wx_ref[dr, h], preferred_element_type=F32) + bx_ref[dr:dr + 1, sl])
            log_a = r * nsp[:, sl]
            a = jnp.exp(log_a)
            a_ref[:, sl] = a
            b_ref[:, sl] = jnp.sqrt(_neg_expm1_2x(log_a, a * a)) * (i * xh)


def _lru_coeffs(z, col_block, conv_w, conv_b, w_a16, b_a, w_x16, b_x, lam, row_len, tm):
    t = z.shape[0]
    d = conv_w.shape[1]
    full = lambda *s: pl.BlockSpec(s, lambda i: (0,) * len(s))
    o_spec = pl.BlockSpec((tm, d), lambda i: (i, 0))
    o_shape = jax.ShapeDtypeStruct((t, d), F32)
    return pl.pallas_call(
        functools.partial(_lru_coef_kernel, row_len),
        grid=(t // tm,),
        in_specs=[pl.BlockSpec((tm, d), lambda i: (i, col_block)),
                  full(*conv_w.shape), full(1, d),
                  full(*w_a16.shape), full(2, d), full(*w_x16.shape), full(2, d), full(2, d)],
        out_specs=[o_spec] * 4,
        out_shape=[o_shape] * 4,
        compiler_params=_params(("arbitrary",)),
        name="lru_coeffs",
    )(z, conv_w, conv_b.reshape(1, d), w_a16, b_a, w_x16, b_x, lam)


def _scan_kernel(reverse, add_prev, a_ref, b_ref, h0_ref, *rest):
    if add_prev:
        y_ref, o_ref, st_ref = rest
    else:
        o_ref, st_ref = rest
    ts, cw = a_ref.shape
    ng = ts // SUBLANES

    @pl.when(pl.program_id(2) == 0)
    def _():
        st_ref[...] = jnp.broadcast_to(h0_ref[...], st_ref.shape)

    row = lax.broadcasted_iota(jnp.int32, (SUBLANES, cw), 0)

    def body(gi, h):
        g = (ng - 1 - gi) if reverse else gi
        off = pl.multiple_of(g * SUBLANES, SUBLANES)
        a = a_ref[pl.ds(off, SUBLANES), :]
        b = b_ref[pl.ds(off, SUBLANES), :]
        for s in (1, 2, 4):
            if reverse:
                shift, m = SUBLANES - s, row < SUBLANES - s
            else:
                shift, m = s, row >= s
            a_s = jnp.where(m, pltpu.roll(a, shift, axis=0), 1.0)
            b_s = jnp.where(m, pltpu.roll(b, shift, axis=0), 0.0)
            b = a * b_s + b
            a = a * a_s
        hh = a * h + b
        if add_prev:
            o_ref[pl.ds(off, SUBLANES), :] = hh + y_ref[pl.ds(off, SUBLANES), :]
        else:
            o_ref[pl.ds(off, SUBLANES), :] = hh
        last = hh[0:1, :] if reverse else hh[SUBLANES - 1:SUBLANES, :]
        return jnp.broadcast_to(last, (SUBLANES, cw))

    st_ref[...] = lax.fori_loop(0, ng, body, st_ref[...], unroll=2)


def _linear_scan(a, b, h0, reverse, prev=None, ts=1024, cw=512):
    bsz, s, c = a.shape
    ts = min(ts, s)
    ns = s // ts
    tmap = (lambda bb, cc, j: (bb, ns - 1 - j, cc)) if reverse else (lambda bb, cc, j: (bb, j, cc))
    blk = pl.BlockSpec((None, ts, cw), tmap)
    ins = [a, b, h0]
    in_specs = [blk, blk, pl.BlockSpec((None, 1, cw), lambda bb, cc, j: (bb, 0, cc))]
    if prev is not None:
        ins.append(prev)
        in_specs.append(blk)
    return pl.pallas_call(
        functools.partial(_scan_kernel, reverse, prev is not None),
        grid=(bsz, c // cw, ns),
        in_specs=in_specs,
        out_specs=blk,
        out_shape=jax.ShapeDtypeStruct((bsz, s, c), F32),
        scratch_shapes=[pltpu.VMEM((SUBLANES, cw), F32)],
        compiler_params=_params(("arbitrary", "arbitrary", "arbitrary")),
        name="linear_scan",
    )(*ins)


def _lru_out_kernel(gate_ref, y_ref, x_ref, g1_ref, w_ref, o_ref):
    p = _gelu_tanh(gate_ref[...].astype(F32)) * y_ref[...]
    out = jnp.dot(p.astype(BF16), w_ref[...], preferred_element_type=F32)
    o_ref[...] = x_ref[...] + g1_ref[0] * out


def _lru_out(z, y, x, g1, w16, rows_per_batch, tm=512):
    t, d = x.shape
    bpb = rows_per_batch // tm
    row = pl.BlockSpec((tm, d), lambda i: (i, 0))
    return pl.pallas_call(
        _lru_out_kernel,
        grid=(t // tm,),
        in_specs=[row, row, row,
                  pl.BlockSpec((1, 1, d), lambda i: (i // bpb, 0, 0)),
                  pl.BlockSpec((d, d), lambda i: (0, 0), pipeline_mode=pl.Buffered(1))],
        out_specs=row,
        out_shape=jax.ShapeDtypeStruct((t, d), F32),
        compiler_params=_params(("arbitrary",)),
        name="lru_out",
    )(z, y, x, g1, w16)


def _sc_out_kernel(bg_ref, cg_ref, v_ref, cw_ref, cb_ref, x_ref, g1_ref, w_ref, o_ref):
    taps = [cw_ref[k:k + 1, :] for k in range(cw_ref.shape[0])]
    y = _row_conv(cg_ref[...].astype(F32) * v_ref[...].astype(F32), taps, cb_ref[...], 1, GRID_W)
    p = bg_ref[...].astype(F32) * y
    out = jnp.dot(p.astype(BF16), w_ref[...], preferred_element_type=F32)
    o_ref[...] = x_ref[...] + g1_ref[0] * out


def _sc_out(z, conv_w, conv_b, x, g1, w16, rows_per_batch, tm=512):
    t, d = x.shape
    bpb = rows_per_batch // tm
    row = pl.BlockSpec((tm, d), lambda i: (i, 0))
    col = lambda c: pl.BlockSpec((tm, d), lambda i: (i, c))
    return pl.pallas_call(
        _sc_out_kernel,
        grid=(t // tm,),
        in_specs=[col(0), col(1), col(2),
                  pl.BlockSpec(conv_w.shape, lambda i: (0, 0)),
                  pl.BlockSpec((1, d), lambda i: (0, 0)),
                  row,
                  pl.BlockSpec((1, 1, d), lambda i: (i // bpb, 0, 0)),
                  pl.BlockSpec((d, d), lambda i: (0, 0), pipeline_mode=pl.Buffered(1))],
        out_specs=row,
        out_shape=jax.ShapeDtypeStruct((t, d), F32),
        compiler_params=_params(("arbitrary",)),
        name="sc_out",
    )(z, z, z, conv_w, conv_b.reshape(1, d), x, g1, w16)


def _top_rows(vals, idx_f, fill, n_out, emit):
    for r in range(n_out):
        m = jnp.max(vals, axis=0, keepdims=True)
        am = jnp.min(jnp.where(vals == m, idx_f, fill), axis=0, keepdims=True)
        emit(r, m, am)
        if r + 1 < n_out:
            vals = jnp.where(idx_f == am, -jnp.inf, vals)


def _route_kernel(q_ref, k_ref, ids_ref, gate_ref, sv_ref, si_ref, cv_ref, cand_ref, eid_ref, flat_ref):
    nk, dk = k_ref.shape[1], k_ref.shape[2]
    kk = PEER_TOPK
    n_cand = cand_ref.shape[0]
    tb = q_ref.shape[0]
    key_idx = lax.broadcasted_iota(jnp.int32, (nk, tb), 0).astype(F32)
    for p in range(2):
        s_t = lax.dot_general(k_ref[p], q_ref[:, p * dk:(p + 1) * dk], (((1,), (1,)), ((), ())),
                              precision=HIGHEST, preferred_element_type=F32)

        def emit(r, m, am, p=p):
            sv_ref[p, r:r + 1, :] = m
            si_ref[p, r:r + 1, :] = am

        _top_rows(s_t, key_idx, float(nk), kk, emit)

    sv1, si1 = sv_ref[1], si_ref[1]
    off = 0
    for i in range(kk):
        n_i = kk // (i + 1)
        cand_ref[off:off + n_i, :] = sv_ref[0, i:i + 1, :] + sv1[0:n_i, :]
        eid_ref[off:off + n_i, :] = si_ref[0, i:i + 1, :] * float(nk) + si1[0:n_i, :]
        flat_ref[off:off + n_i, :] = lax.broadcasted_iota(jnp.int32, (n_i, tb), 0).astype(F32) + float(i * kk)
        off += n_i
    cand_ref[off:n_cand, :] = jnp.full((n_cand - off, tb), -jnp.inf, F32)
    eid_ref[off:n_cand, :] = jnp.zeros((n_cand - off, tb), F32)
    flat_ref[off:n_cand, :] = jnp.full((n_cand - off, tb), float(kk * kk), F32)
    eid, flat = eid_ref[...], flat_ref[...]

    def emit2(r, m, am):
        e = jnp.max(jnp.where(flat == am, eid, -1.0), axis=0, keepdims=True)
        cv_ref[r:r + 1, :] = m
        ids_ref[r:r + 1, :] = e.astype(jnp.int32)

    _top_rows(cand_ref[...], flat, float(kk * kk), kk, emit2)
    cv = cv_ref[...]
    ex = jnp.exp(cv - jnp.max(cv, axis=0, keepdims=True))
    gate_ref[...] = ex / jnp.sum(ex, axis=0, keepdims=True)


def _peer_route(q, sub_keys, tb=1024):
    t = q.shape[0]
    nh, _, nk, dk = sub_keys.shape
    kk = PEER_TOPK
    n_cand = sum(kk // (i + 1) for i in range(kk))
    n_cand += (-n_cand) % SUBLANES
    o_spec = pl.BlockSpec((kk, tb), lambda i, h: (h, i))
    return pl.pallas_call(
        _route_kernel,
        grid=(t // tb, nh),
        in_specs=[pl.BlockSpec((tb, 2 * dk), lambda i, h: (i, h)),
                  pl.BlockSpec((None, 2, nk, dk), lambda i, h: (h, 0, 0, 0))],
        out_specs=[o_spec, o_spec],
        out_shape=[jax.ShapeDtypeStruct((nh * kk, t), jnp.int32),
                   jax.ShapeDtypeStruct((nh * kk, t), F32)],
        scratch_shapes=[pltpu.VMEM((2, kk, tb), F32), pltpu.VMEM((2, kk, tb), F32), pltpu.VMEM((kk, tb), F32)]
        + [pltpu.VMEM((n_cand, tb), F32)] * 3,
        compiler_params=_params(("arbitrary", "arbitrary")),
        name="peer_route",
    )(q, sub_keys)


def _pack_pair(x):
    half = x.shape[1] // 2
    bits = pltpu.bitcast(x.astype(BF16).astype(F32), jnp.uint32)
    return bits[:, half:] | (bits[:, :half] >> 16)


def _unpack_pair(w):
    return pltpu.bitcast(w << 16, F32), pltpu.bitcast(w & jnp.uint32(0xFFFF0000), F32)


def _pack_kernel(u_ref, v_ref, o_ref):
    half = u_ref.shape[1] // 2
    o_ref[:, 0, :half] = _pack_pair(u_ref[...])
    o_ref[:, 0, half:] = _pack_pair(v_ref[...])


def _pack_experts(u_all, v_all, layer, te=512):
    _, e, d = u_all.shape
    src = pl.BlockSpec((None, te, d), lambda i: (layer, i, 0))
    return pl.pallas_call(
        _pack_kernel,
        grid=(e // te,),
        in_specs=[src, src],
        out_specs=pl.BlockSpec((te, 1, d), lambda i: (i, 0, 0)),
        out_shape=jax.ShapeDtypeStruct((e, 1, d), jnp.uint32),
        compiler_params=_params(("arbitrary",)),
        name="pack_experts",
    )(u_all, v_all)


PEER_TOKENS_PER_STEP = 128
PEER_SLOTS = SUBLANES


def _peer_kernel(final_norm, project, head_ref, ids_ref, gate_ref, h_ref, x_ref, g2_ref, fin_ref, tab_ref, *rest):
    if project:
        pg_ref, psh_ref, psc_ref, pw_ref, o_ref, z_ref, buf_ref, orow_ref, sem_ref = rest
    else:
        o_ref, buf_ref, orow_ref, sem_ref = rest
    tb, d = h_ref.shape
    ne = PEER_PER_TOKEN
    nslots = buf_ref.shape[0]
    nword = d // (2 * LANES)
    ahead = nslots - 1
    step, last_step = pl.program_id(0), pl.num_programs(0) - 1

    def start_rows(src_ids, tok, slot, part=0, nparts=1):
        for k in range(part * ne // nparts, (part + 1) * ne // nparts):
            pltpu.make_async_copy(tab_ref.at[src_ids[tok, k]], buf_ref.at[slot, pl.ds(k, 1), :],
                                  sem_ref.at[slot]).start(priority=k % 2)

    def wait_token(slot):
        pltpu.make_async_copy(tab_ref.at[pl.ds(0, ne), 0], buf_ref.at[slot], sem_ref.at[slot]).wait()

    lane_tok = lax.broadcasted_iota(jnp.int32, (ne, tb), 1)

    def token(base, j, issue_next):
        wait_token(j)
        rows = pl.ds(base, nslots)
        acc = jnp.zeros((ne, LANES), F32)
        for c in range(nword):
            lo, hi = _unpack_pair(buf_ref[j, :, c * LANES:(c + 1) * LANES])
            acc = acc + lo * h_ref[rows, c * LANES:(c + 1) * LANES][j:j + 1, :]
            acc = acc + hi * h_ref[rows, (nword + c) * LANES:(nword + c + 1) * LANES][j:j + 1, :]
            issue_next(c, 2 * nword)
        act = _gelu_tanh(jnp.sum(acc, axis=1, keepdims=True))
        gcol = jnp.sum(jnp.where(lane_tok == base + j, gate_ref[...], 0.0), axis=1, keepdims=True)
        wb = jnp.broadcast_to(gcol * act, (ne, LANES))
        for c in range(nword):
            lo, hi = _unpack_pair(buf_ref[j, :, (nword + c) * LANES:(nword + c + 1) * LANES])
            orow_ref[j:j + 1, c * LANES:(c + 1) * LANES] = jnp.sum(lo * wb, axis=0, keepdims=True)
            orow_ref[j:j + 1, (nword + c) * LANES:(nword + c + 1) * LANES] = jnp.sum(hi * wb, axis=0, keepdims=True)
            issue_next(nword + c, 2 * nword)

    def finish_group(base):
        rows = pl.ds(base, nslots)
        y = x_ref[rows, :] + g2_ref[0] * orow_ref[...]
        if final_norm:
            y = (y * lax.rsqrt(jnp.mean(y * y, axis=-1, keepdims=True) + EPS)) * fin_ref[...]
        o_ref[rows, :] = y

    @pl.when(step == 0)
    def _():
        def first(s, carry):
            start_rows(head_ref, s, s)
            return carry
        lax.fori_loop(0, ahead, first, 0)

    def group(g, carry):
        base = pl.multiple_of(g * nslots, nslots)
        for j in range(nslots):
            token(base, j, functools.partial(start_rows, ids_ref, base + j, (j + ahead) % nslots))
        finish_group(base)
        return carry

    lax.fori_loop(0, tb // nslots, group, 0)

    if project:
        y = o_ref[...]
        hn = (y * lax.rsqrt(jnp.mean(y * y, axis=-1, keepdims=True) + EPS)) * pg_ref[...]
        h16 = (hn * (1.0 + psc_ref[0]) + psh_ref[0]).astype(BF16)
        tn = 512
        for j in range(pw_ref.shape[1] // tn):
            cols = slice(j * tn, (j + 1) * tn)
            z_ref[:, cols] = jnp.dot(h16, pw_ref[:, cols], preferred_element_type=F32).astype(z_ref.dtype)

    @pl.when(step == last_step)
    def _():
        for s in range(ahead):
            wait_token(s)


def _peer_experts(ids_t, gates, h, x, g2, table, rows_per_batch, final_gamma, final_norm, next_proj=None):
    t, d = x.shape
    tb = PEER_TOKENS_PER_STEP
    ne = PEER_PER_TOKEN
    bpb = rows_per_batch // tb
    ahead = PEER_SLOTS - 1
    ids_ahead = jnp.concatenate([ids_t[ahead:], jnp.zeros((ahead, ne), jnp.int32)], axis=0)
    row = pl.BlockSpec((tb, d), lambda i: (i, 0))
    per_batch = pl.BlockSpec((1, 1, d), lambda i: (i // bpb, 0, 0))
    in_specs = [pl.BlockSpec((PEER_SLOTS, ne), lambda i: (0, 0), memory_space=pltpu.SMEM),
                pl.BlockSpec((tb, ne), lambda i: (i, 0), memory_space=pltpu.SMEM),
                pl.BlockSpec((ne, tb), lambda i: (0, i)),
                row, row, per_batch,
                pl.BlockSpec((1, d), lambda i: (0, 0)),
                pl.BlockSpec(memory_space=pl.ANY)]
    args = [ids_t[:PEER_SLOTS], ids_ahead, gates, h, x, g2, final_gamma.reshape(1, d), table]
    out_specs, out_shape = [row], [jax.ShapeDtypeStruct((t, d), F32)]
    if next_proj is not None:
        p_gamma, p_shift, p_scale, p_w = next_proj
        n = p_w.shape[1]
        in_specs += [pl.BlockSpec((1, d), lambda i: (0, 0)), per_batch, per_batch,
                     pl.BlockSpec((d, n), lambda i: (0, 0), pipeline_mode=pl.Buffered(1))]
        args += [p_gamma.reshape(1, d), p_shift, p_scale, p_w]
        out_specs.append(pl.BlockSpec((tb, n), lambda i: (i, 0)))
        out_shape.append(jax.ShapeDtypeStruct((t, n), BF16))
    res = pl.pallas_call(
        functools.partial(_peer_kernel, final_norm, next_proj is not None),
        grid=(t // tb,),
        in_specs=in_specs,
        out_specs=out_specs,
        out_shape=out_shape,
        scratch_shapes=[pltpu.VMEM((PEER_SLOTS, ne, d), jnp.uint32),
                        pltpu.VMEM((PEER_SLOTS, d), F32),
                        pltpu.SemaphoreType.DMA((PEER_SLOTS,))],
        compiler_params=_params(("arbitrary",)),
        name="peer_experts",
    )(*args)
    return res if next_proj is not None else res[0]


def _peer_layer(x, mod, gamma, w_q, sub_keys, u_all, v_all, layer, seq, final_gamma, final_norm, next_proj=None):
    bsz, d = mod.shape[0], x.shape[1]
    sh2 = mod[:, 3 * d:4 * d].reshape(bsz, 1, d)
    sc2 = mod[:, 4 * d:5 * d].reshape(bsz, 1, d)
    g2 = mod[:, 5 * d:6 * d].reshape(bsz, 1, d)
    q, h = _norm_mod_matmul(x, gamma, sh2, sc2, w_q.astype(BF16), seq, emit_h=True)
    ids, gates = _peer_route(q, sub_keys)
    table = _pack_experts(u_all, v_all, layer)
    return _peer_experts(ids.T, gates, h, x, g2, table, seq, final_gamma, final_norm, next_proj)


def kernel(x, c, ctx, c_ctx, w_mod, b_mod, norm_mix_g, norm_ffn_g, norm_final_g, lru_w_in, lru_conv_w, lru_conv_b, lru_w_a, lru_b_a, lru_w_x, lru_b_x, lru_lambda, lru_w_out, sc_w_in, sc_conv_w, sc_conv_b, sc_w_out, peer_w_q, peer_sub_keys, peer_u, peer_v):
    bsz, seq, d = x.shape
    ctx_len = ctx.shape[1]
    xt = x.reshape(bsz * seq, d)

    c_rows = jnp.concatenate([c, c_ctx[None, :], jnp.zeros((SUBLANES - bsz - 1, d), F32)], axis=0)
    mods = [_mod_vectors(c_rows, w_mod, b_mod, i) for i in range(w_mod.shape[0])]

    mod = mods[0][:bsz]
    sh1, sc1, g1 = (mod[:, k * d:(k + 1) * d].reshape(bsz, 1, d) for k in range(3))
    w_in16 = lru_w_in[0].astype(BF16)
    w_a16, w_x16 = lru_w_a[0].astype(BF16), lru_w_x[0].astype(BF16)
    z = _norm_mod_matmul(xt, norm_mix_g[0], sh1, sc1, w_in16, seq, out_dtype=BF16)

    mod_c = mods[0][bsz:bsz + 1]
    sh_c = jnp.broadcast_to(mod_c[:, 0:d].reshape(1, 1, d), (bsz, 1, d))
    sc_c = jnp.broadcast_to(mod_c[:, d:2 * d].reshape(1, 1, d), (bsz, 1, d))
    z_ctx = _norm_mod_matmul(ctx.reshape(bsz * ctx_len, d), norm_mix_g[0], sh_c, sc_c, w_in16, ctx_len,
                             out_dtype=BF16)
    coef_args = (1, lru_conv_w[0], lru_conv_b[0], w_a16, lru_b_a[0], w_x16, lru_b_x[0], lru_lambda[0])
    cf = [t.reshape(bsz, ctx_len, d) for t in _lru_coeffs(z_ctx, *coef_args, ctx_len, ctx_len)]
    zero_state = jnp.zeros((bsz, 1, d), F32)
    hc_f = _linear_scan(cf[0], cf[1], zero_state, False)
    hc_r = _linear_scan(cf[2], cf[3], zero_state, True)

    lf = [t.reshape(bsz, seq, d) for t in _lru_coeffs(z, *coef_args, GRID_W, 256)]
    h_f = _linear_scan(lf[0], lf[1], hc_f[:, ctx_len - 1:ctx_len, :], False)
    y = _linear_scan(lf[2], lf[3], hc_r[:, 0:1, :], True, prev=h_f)
    xt = _lru_out(z, y.reshape(bsz * seq, d), xt, g1, lru_w_out[0].astype(BF16), seq)
    mod1 = mods[1][:bsz]
    sh1, sc1, g1 = (mod1[:, k * d:(k + 1) * d].reshape(bsz, 1, d) for k in range(3))
    xt, z = _peer_layer(xt, mod, norm_ffn_g[0], peer_w_q[0], peer_sub_keys[0], peer_u, peer_v, 0, seq,
                        norm_final_g, False, next_proj=(norm_mix_g[1], sh1, sc1, sc_w_in[0].astype(BF16)))
    mod = mod1
    xt = _sc_out(z, sc_conv_w[0], sc_conv_b[0], xt, g1, sc_w_out[0].astype(BF16), seq)
    xt = _peer_layer(xt, mod, norm_ffn_g[1], peer_w_q[1], peer_sub_keys[1], peer_u, peer_v, 1, seq,
                     norm_final_g, True)
    return xt.reshape(bsz, seq, d)
```

```python
import functools

import jax
import jax.numpy as jnp
from jax import lax
from jax.experimental import pallas as pl
from jax.experimental.pallas import tpu as pltpu

EPS = 1e-6
GRID_W = 64
LRU_HEADS = 16
LRU_C = 8.0
PEER_HEADS = 8
PEER_N_KEYS = 128
PEER_TOPK = 16
PEER_PER_TOKEN = PEER_HEADS * PEER_TOPK

LANES = 128
SUBLANES = 8
VMEM_LIMIT = 58 * 1024 * 1024

F32 = jnp.float32
BF16 = jnp.bfloat16
HIGHEST = lax.Precision.HIGHEST


def _params(sem):
    return pltpu.CompilerParams(dimension_semantics=sem, vmem_limit_bytes=VMEM_LIMIT)


def _gelu_tanh(x):
    c = 0.7978845608028654
    return x * (0.5 * (1.0 + jnp.tanh(c * (x + 0.044715 * (x * x * x)))))


def _neg_expm1_2x(half_z, exp_z):
    return -jnp.tanh(half_z) * (exp_z + 1.0)


def _sigmoid(x):
    return 0.5 * (jnp.tanh(0.5 * x) + 1.0)


def _mod_kernel(c_ref, w_ref, b_ref, o_ref):
    c = c_ref[...]
    s = c * jax.nn.sigmoid(c)
    o_ref[...] = jnp.dot(s, w_ref[...], preferred_element_type=F32, precision=HIGHEST) + b_ref[...]


def _mod_vectors(c_rows, w_all, b_all, layer):
    nl, d, n = w_all.shape
    tn = 1024
    return pl.pallas_call(
        _mod_kernel,
        grid=(n // tn,),
        in_specs=[pl.BlockSpec((SUBLANES, d), lambda j: (0, 0)),
                  pl.BlockSpec((None, d, tn), lambda j: (layer, 0, j)),
                  pl.BlockSpec((None, 1, tn), lambda j: (layer, 0, j))],
        out_specs=pl.BlockSpec((SUBLANES, tn), lambda j: (0, j)),
        out_shape=jax.ShapeDtypeStruct((SUBLANES, n), F32),
        compiler_params=_params(("arbitrary",)),
        name="mod_vectors",
    )(c_rows, w_all, b_all.reshape(nl, 1, n))


def _nmm_kernel(emit_h, x_ref, g_ref, sh_ref, sc_ref, w_ref, o_ref, *rest):
    if emit_h:
        h_ref, hn_ref = rest
    else:
        (hn_ref,) = rest

    @pl.when(pl.program_id(1) == 0)
    def _():
        x = x_ref[...]
        ms = jnp.mean(x * x, axis=-1, keepdims=True)
        y = (x * lax.rsqrt(ms + EPS)) * g_ref[...]
        h = y * (1.0 + sc_ref[0]) + sh_ref[0]
        hn_ref[...] = h.astype(BF16)
        if emit_h:
            h_ref[...] = h

    o_ref[...] = jnp.dot(hn_ref[...], w_ref[...], preferred_element_type=F32).astype(o_ref.dtype)


def _norm_mod_matmul(x, gamma, shift, scale, w_bf16, rows_per_batch, emit_h=False, out_dtype=F32):
    t, d = x.shape
    n = w_bf16.shape[1]
    tn = 1024
    tm = min(512 if emit_h else 1024, rows_per_batch)
    bpb = rows_per_batch // tm
    out_shape = [jax.ShapeDtypeStruct((t, n), out_dtype)]
    out_specs = [pl.BlockSpec((tm, tn), lambda i, j: (i, j))]
    if emit_h:
        out_shape.append(jax.ShapeDtypeStruct((t, d), F32))
        out_specs.append(pl.BlockSpec((tm, d), lambda i, j: (i, 0)))
    res = pl.pallas_call(
        functools.partial(_nmm_kernel, emit_h),
        grid=(t // tm, n // tn),
        in_specs=[pl.BlockSpec((tm, d), lambda i, j: (i, 0)),
                  pl.BlockSpec((1, d), lambda i, j: (0, 0)),
                  pl.BlockSpec((1, 1, d), lambda i, j: (i // bpb, 0, 0)),
                  pl.BlockSpec((1, 1, d), lambda i, j: (i // bpb, 0, 0)),
                  pl.BlockSpec((d, tn), lambda i, j: (0, j))],
        out_specs=out_specs,
        out_shape=out_shape,
        scratch_shapes=[pltpu.VMEM((tm, d), BF16)],
        compiler_params=_params(("arbitrary", "arbitrary")),
        name="norm_mod_matmul",
    )(x, gamma.reshape(1, d), shift, scale, w_bf16)
    return res if emit_h else res[0]


def _row_conv(u, taps, bias, pad_left, row_len):
    tm = u.shape[0]
    pos = lax.broadcasted_iota(jnp.int32, (tm, 1), 0) % row_len
    y = None
    for k in range(len(taps)):
        off = k - pad_left
        if off == 0:
            term = u
        else:
            rolled = pltpu.roll(u, (-off) % tm, axis=0)
            valid = jnp.logical_and(pos + off >= 0, pos + off < row_len)
            term = jnp.where(valid, rolled, 0.0)
        y = (bias + taps[k] * term) if y is None else (y + taps[k] * term)
    return y


def _lru_coef_kernel(row_len, xb_ref, cw_ref, cb_ref, wa_ref, ba_ref, wx_ref, bx_ref, lam_ref,
                     af_ref, bf_ref, ar_ref, br_ref):
    taps = [cw_ref[k:k + 1, :] for k in range(cw_ref.shape[0])]
    xc = _row_conv(xb_ref[...].astype(F32), taps, cb_ref[...], 2, row_len)
    d = xc.shape[1]
    hd = d // LRU_HEADS
    outs = ((af_ref, bf_ref), (ar_ref, br_ref))
    for dr in range(2):
        lam = lam_ref[dr:dr + 1, :]
        z = -lam
        nsp = (-LRU_C) * (jnp.maximum(z, 0.0) + jnp.log1p(jnp.exp(-jnp.abs(z))))
        a_ref, b_ref = outs[dr]
        for h in range(LRU_HEADS):
            sl = slice(h * hd, (h + 1) * hd)
            xh = xc[:, sl]
            xh16 = xh.astype(BF16)
            r = _sigmoid(jnp.dot(xh16, wa_ref[dr, h], preferred_element_type=F32) + ba_ref[dr:dr + 1, sl])
            i = _sigmoid(jnp.dot(xh16, wx_ref[dr, h], preferred_element_type=F32) + bx_ref[dr:dr + 1, sl])
            log_a = r * nsp[:, sl]
            a = jnp.exp(log_a)
            a_ref[:, sl] = a
            b_ref[:, sl] = jnp.sqrt(_neg_expm1_2x(log_a, a * a)) * (i * xh)


def _lru_coeffs(z, col_block, conv_w, conv_b, w_a16, b_a, w_x16, b_x, lam, row_len, tm):
    t = z.shape[0]
    d = conv_w.shape[1]
    full = lambda *s: pl.BlockSpec(s, lambda i: (0,) * len(s))
    o_spec = pl.BlockSpec((tm, d), lambda i: (i, 0))
    o_shape = jax.ShapeDtypeStruct((t, d), F32)
    return pl.pallas_call(
        functools.partial(_lru_coef_kernel, row_len),
        grid=(t // tm,),
        in_specs=[pl.BlockSpec((tm, d), lambda i: (i, col_block)),
                  full(*conv_w.shape), full(1, d),
                  full(*w_a16.shape), full(2, d), full(*w_x16.shape), full(2, d), full(2, d)],
        out_specs=[o_spec] * 4,
        out_shape=[o_shape] * 4,
        compiler_params=_params(("arbitrary",)),
        name="lru_coeffs",
    )(z, conv_w, conv_b.reshape(1, d), w_a16, b_a, w_x16, b_x, lam)


def _scan_kernel(reverse, add_prev, a_ref, b_ref, h0_ref, *rest):
    if add_prev:
        y_ref, o_ref, st_ref = rest
    else:
        o_ref, st_ref = rest
    ts, cw = a_ref.shape
    ng = ts // SUBLANES

    @pl.when(pl.program_id(2) == 0)
    def _():
        st_ref[...] = jnp.broadcast_to(h0_ref[...], st_ref.shape)

    row = lax.broadcasted_iota(jnp.int32, (SUBLANES, cw), 0)

    def body(gi, h):
        g = (ng - 1 - gi) if reverse else gi
        off = pl.multiple_of(g * SUBLANES, SUBLANES)
        a = a_ref[pl.ds(off, SUBLANES), :]
        b = b_ref[pl.ds(off, SUBLANES), :]
        for s in (1, 2, 4):
            if reverse:
                shift, m = SUBLANES - s, row < SUBLANES - s
            else:
                shift, m = s, row >= s
            a_s = jnp.where(m, pltpu.roll(a, shift, axis=0), 1.0)
            b_s = jnp.where(m, pltpu.roll(b, shift, axis=0), 0.0)
            b = a * b_s + b
            a = a * a_s
        hh = a * h + b
        if add_prev:
            o_ref[pl.ds(off, SUBLANES), :] = hh + y_ref[pl.ds(off, SUBLANES), :]
        else:
            o_ref[pl.ds(off, SUBLANES), :] = hh
        last = hh[0:1, :] if reverse else hh[SUBLANES - 1:SUBLANES, :]
        return jnp.broadcast_to(last, (SUBLANES, cw))

    st_ref[...] = lax.fori_loop(0, ng, body, st_ref[...], unroll=2)


def _linear_scan(a, b, h0, reverse, prev=None, ts=1024, cw=512):
    bsz, s, c = a.shape
    ts = min(ts, s)
    ns = s // ts
    tmap = (lambda bb, cc, j: (bb, ns - 1 - j, cc)) if reverse else (lambda bb, cc, j: (bb, j, cc))
    blk = pl.BlockSpec((None, ts, cw), tmap)
    ins = [a, b, h0]
    in_specs = [blk, blk, pl.BlockSpec((None, 1, cw), lambda bb, cc, j: (bb, 0, cc))]
    if prev is not None:
        ins.append(prev)
        in_specs.append(blk)
    return pl.pallas_call(
        functools.partial(_scan_kernel, reverse, prev is not None),
        grid=(bsz, c // cw, ns),
        in_specs=in_specs,
        out_specs=blk,
        out_shape=jax.ShapeDtypeStruct((bsz, s, c), F32),
        scratch_shapes=[pltpu.VMEM((SUBLANES, cw), F32)],
        compiler_params=_params(("arbitrary", "arbitrary", "arbitrary")),
        name="linear_scan",
    )(*ins)


def _lru_out_kernel(gate_ref, y_ref, x_ref, g1_ref, w_ref, o_ref):
    p = _gelu_tanh(gate_ref[...].astype(F32)) * y_ref[...]
    out = jnp.dot(p.astype(BF16), w_ref[...], preferred_element_type=F32)
    o_ref[...] = x_ref[...] + g1_ref[0] * out


def _lru_out(z, y, x, g1, w16, rows_per_batch, tm=512):
    t, d = x.shape
    bpb = rows_per_batch // tm
    row = pl.BlockSpec((tm, d), lambda i: (i, 0))
    return pl.pallas_call(
        _lru_out_kernel,
        grid=(t // tm,),
        in_specs=[row, row, row,
                  pl.BlockSpec((1, 1, d), lambda i: (i // bpb, 0, 0)),
                  pl.BlockSpec((d, d), lambda i: (0, 0), pipeline_mode=pl.Buffered(1))],
        out_specs=row,
        out_shape=jax.ShapeDtypeStruct((t, d), F32),
        compiler_params=_params(("arbitrary",)),
        name="lru_out",
    )(z, y, x, g1, w16)


def _sc_out_kernel(bg_ref, cg_ref, v_ref, cw_ref, cb_ref, x_ref, g1_ref, w_ref, o_ref):
    taps = [cw_ref[k:k + 1, :] for k in range(cw_ref.shape[0])]
    y = _row_conv(cg_ref[...].astype(F32) * v_ref[...].astype(F32), taps, cb_ref[...], 1, GRID_W)
    p = bg_ref[...].astype(F32) * y
    out = jnp.dot(p.astype(BF16), w_ref[...], preferred_element_type=F32)
    o_ref[...] = x_ref[...] + g1_ref[0] * out


def _sc_out(z, conv_w, conv_b, x, g1, w16, rows_per_batch, tm=512):
    t, d = x.shape
    bpb = rows_per_batch // tm
    row = pl.BlockSpec((tm, d), lambda i: (i, 0))
    col = lambda c: pl.BlockSpec((tm, d), lambda i: (i, c))
    return pl.pallas_call(
        _sc_out_kernel,
        grid=(t // tm,),
        in_specs=[col(0), col(1), col(2),
                  pl.BlockSpec(conv_w.shape, lambda i: (0, 0)),
                  pl.BlockSpec((1, d), lambda i: (0, 0)),
                  row,
                  pl.BlockSpec((1, 1, d), lambda i: (i // bpb, 0, 0)),
                  pl.BlockSpec((d, d), lambda i: (0, 0), pipeline_mode=pl.Buffered(1))],
        out_specs=row,
        out_shape=jax.ShapeDtypeStruct((t, d), F32),
        compiler_params=_params(("arbitrary",)),
        name="sc_out",
    )(z, z, z, conv_w, conv_b.reshape(1, d), x, g1, w16)


def _top_rows(vals, idx_f, fill, n_out, emit):
    for r in range(n_out):
        m = jnp.max(vals, axis=0, keepdims=True)
        am = jnp.min(jnp.where(vals == m, idx_f, fill), axis=0, keepdims=True)
        emit(r, m, am)
        if r + 1 < n_out:
            vals = jnp.where(idx_f == am, -jnp.inf, vals)


def _route_kernel(q_ref, k_ref, ids_ref, gate_ref, sv_ref, si_ref, cv_ref, cand_ref, eid_ref, flat_ref):
    nk, dk = k_ref.shape[1], k_ref.shape[2]
    kk = PEER_TOPK
    n_cand = cand_ref.shape[0]
    tb = q_ref.shape[0]
    key_idx = lax.broadcasted_iota(jnp.int32, (nk, tb), 0).astype(F32)
    for p in range(2):
        s_t = lax.dot_general(k_ref[p], q_ref[:, p * dk:(p + 1) * dk], (((1,), (1,)), ((), ())),
                              precision=HIGHEST, preferred_element_type=F32)

        def emit(r, m, am, p=p):
            sv_ref[p, r:r + 1, :] = m
            si_ref[p, r:r + 1, :] = am

        _top_rows(s_t, key_idx, float(nk), kk, emit)

    sv1, si1 = sv_ref[1], si_ref[1]
    off = 0
    for i in range(kk):
        n_i = kk // (i + 1)
        cand_ref[off:off + n_i, :] = sv_ref[0, i:i + 1, :] + sv1[0:n_i, :]
        eid_ref[off:off + n_i, :] = si_ref[0, i:i + 1, :] * float(nk) + si1[0:n_i, :]
        flat_ref[off:off + n_i, :] = lax.broadcasted_iota(jnp.int32, (n_i, tb), 0).astype(F32) + float(i * kk)
        off += n_i
    cand_ref[off:n_cand, :] = jnp.full((n_cand - off, tb), -jnp.inf, F32)
    eid_ref[off:n_cand, :] = jnp.zeros((n_cand - off, tb), F32)
    flat_ref[off:n_cand, :] = jnp.full((n_cand - off, tb), float(kk * kk), F32)
    eid, flat = eid_ref[...], flat_ref[...]

    def emit2(r, m, am):
        e = jnp.max(jnp.where(flat == am, eid, -1.0), axis=0, keepdims=True)
        cv_ref[r:r + 1, :] = m
        ids_ref[r:r + 1, :] = e.astype(jnp.int32)

    _top_rows(cand_ref[...], flat, float(kk * kk), kk, emit2)
    cv = cv_ref[...]
    ex = jnp.exp(cv - jnp.max(cv, axis=0, keepdims=True))
    gate_ref[...] = ex / jnp.sum(ex, axis=0, keepdims=True)


def _peer_route(q, sub_keys, tb=1024):
    t = q.shape[0]
    nh, _, nk, dk = sub_keys.shape
    kk = PEER_TOPK
    n_cand = sum(kk // (i + 1) for i in range(kk))
    n_cand += (-n_cand) % SUBLANES
    o_spec = pl.BlockSpec((kk, tb), lambda i, h: (h, i))
    return pl.pallas_call(
        _route_kernel,
        grid=(t // tb, nh),
        in_specs=[pl.BlockSpec((tb, 2 * dk), lambda i, h: (i, h)),
                  pl.BlockSpec((None, 2, nk, dk), lambda i, h: (h, 0, 0, 0))],
        out_specs=[o_spec, o_spec],
        out_shape=[jax.ShapeDtypeStruct((nh * kk, t), jnp.int32),
                   jax.ShapeDtypeStruct((nh * kk, t), F32)],
        scratch_shapes=[pltpu.VMEM((2, kk, tb), F32), pltpu.VMEM((2, kk, tb), F32), pltpu.VMEM((kk, tb), F32)]
        + [pltpu.VMEM((n_cand, tb), F32)] * 3,
        compiler_params=_params(("arbitrary", "arbitrary")),
        name="peer_route",
    )(q, sub_keys)


def _pack_pair(x):
    half = x.shape[1] // 2
    bits = pltpu.bitcast(x.astype(BF16).astype(F32), jnp.uint32)
    return bits[:, half:] | (bits[:, :half] >> 16)


def _unpack_pair(w):
    return pltpu.bitcast(w << 16, F32), pltpu.bitcast(w & jnp.uint32(0xFFFF0000), F32)


def _pack_kernel(u_ref, v_ref, o_ref):
    half = u_ref.shape[1] // 2
    o_ref[:, 0, :half] = _pack_pair(u_ref[...])
    o_ref[:, 0, half:] = _pack_pair(v_ref[...])


def _pack_experts(u_all, v_all, layer, te=512):
    _, e, d = u_all.shape
    src = pl.BlockSpec((None, te, d), lambda i: (layer, i, 0))
    return pl.pallas_call(
        _pack_kernel,
        grid=(e // te,),
        in_specs=[src, src],
        out_specs=pl.BlockSpec((te, 1, d), lambda i: (i, 0, 0)),
        out_shape=jax.ShapeDtypeStruct((e, 1, d), jnp.uint32),
        compiler_params=_params(("arbitrary",)),
        name="pack_experts",
    )(u_all, v_all)


PEER_TOKENS_PER_STEP = 128
PEER_SLOTS = 2 * SUBLANES


def _peer_kernel(final_norm, project, head_ref, ids_ref, gate_ref, h_ref, x_ref, g2_ref, fin_ref, tab_ref, *rest):
    if project:
        pg_ref, psh_ref, psc_ref, pw_ref, o_ref, z_ref, buf_ref, orow_ref, sem_ref = rest
    else:
        o_ref, buf_ref, orow_ref, sem_ref = rest
    tb, d = h_ref.shape
    ne = PEER_PER_TOKEN
    nslots = buf_ref.shape[0]
    nword = d // (2 * LANES)
    ahead = nslots - 1
    step, last_step = pl.program_id(0), pl.num_programs(0) - 1

    def start_rows(src_ids, tok, slot, part=0, nparts=1):
        for k in range(part * ne // nparts, (part + 1) * ne // nparts):
            pltpu.make_async_copy(tab_ref.at[src_ids[tok, k]], buf_ref.at[slot, pl.ds(k, 1), :],
                                  sem_ref.at[slot]).start(priority=k % 2)

    def wait_token(slot):
        pltpu.make_async_copy(tab_ref.at[pl.ds(0, ne), 0], buf_ref.at[slot], sem_ref.at[slot]).wait()

    lane_tok = lax.broadcasted_iota(jnp.int32, (ne, tb), 1)

    def token(base, j, issue_next):
        wait_token(j)
        rows = pl.ds(base, nslots)
        acc = jnp.zeros((ne, LANES), F32)
        for c in range(nword):
            lo, hi = _unpack_pair(buf_ref[j, :, c * LANES:(c + 1) * LANES])
            acc = acc + lo * h_ref[rows, c * LANES:(c + 1) * LANES][j:j + 1, :]
            acc = acc + hi * h_ref[rows, (nword + c) * LANES:(nword + c + 1) * LANES][j:j + 1, :]
            issue_next(c, 2 * nword)
        act = _gelu_tanh(jnp.sum(acc, axis=1, keepdims=True))
        gcol = jnp.sum(jnp.where(lane_tok == base + j, gate_ref[...], 0.0), axis=1, keepdims=True)
        wb = jnp.broadcast_to(gcol * act, (ne, LANES))
        for c in range(nword):
            lo, hi = _unpack_pair(buf_ref[j, :, (nword + c) * LANES:(nword + c + 1) * LANES])
            orow_ref[j:j + 1, c * LANES:(c + 1) * LANES] = jnp.sum(lo * wb, axis=0, keepdims=True)
            orow_ref[j:j + 1, (nword + c) * LANES:(nword + c + 1) * LANES] = jnp.sum(hi * wb, axis=0, keepdims=True)
            issue_next(nword + c, 2 * nword)

    def finish_group(base):
        rows = pl.ds(base, nslots)
        y = x_ref[rows, :] + g2_ref[0] * orow_ref[...]
        if final_norm:
            y = (y * lax.rsqrt(jnp.mean(y * y, axis=-1, keepdims=True) + EPS)) * fin_ref[...]
        o_ref[rows, :] = y

    @pl.when(step == 0)
    def _():
        def first(s, carry):
            start_rows(head_ref, s, s)
            return carry
        lax.fori_loop(0, ahead, first, 0)

    def group(g, carry):
        base = pl.multiple_of(g * nslots, nslots)
        for j in range(nslots):
            token(base, j, functools.partial(start_rows, ids_ref, base + j, (j + ahead) % nslots))
        finish_group(base)
        return carry

    lax.fori_loop(0, tb // nslots, group, 0)

    if project:
        y = o_ref[...]
        hn = (y * lax.rsqrt(jnp.mean(y * y, axis=-1, keepdims=True) + EPS)) * pg_ref[...]
        h16 = (hn * (1.0 + psc_ref[0]) + psh_ref[0]).astype(BF16)
        tn = 512
        for j in range(pw_ref.shape[1] // tn):
            cols = slice(j * tn, (j + 1) * tn)
            z_ref[:, cols] = jnp.dot(h16, pw_ref[:, cols], preferred_element_type=F32).astype(z_ref.dtype)

    @pl.when(step == last_step)
    def _():
        for s in range(ahead):
            wait_token(s)


def _peer_experts(ids_t, gates, h, x, g2, table, rows_per_batch, final_gamma, final_norm, next_proj=None):
    t, d = x.shape
    tb = PEER_TOKENS_PER_STEP
    ne = PEER_PER_TOKEN
    bpb = rows_per_batch // tb
    ahead = PEER_SLOTS - 1
    ids_ahead = jnp.concatenate([ids_t[ahead:], jnp.zeros((ahead, ne), jnp.int32)], axis=0)
    row = pl.BlockSpec((tb, d), lambda i: (i, 0))
    per_batch = pl.BlockSpec((1, 1, d), lambda i: (i // bpb, 0, 0))
    in_specs = [pl.BlockSpec((PEER_SLOTS, ne), lambda i: (0, 0), memory_space=pltpu.SMEM),
                pl.BlockSpec((tb, ne), lambda i: (i, 0), memory_space=pltpu.SMEM),
                pl.BlockSpec((ne, tb), lambda i: (0, i)),
                row, row, per_batch,
                pl.BlockSpec((1, d), lambda i: (0, 0)),
                pl.BlockSpec(memory_space=pl.ANY)]
    args = [ids_t[:PEER_SLOTS], ids_ahead, gates, h, x, g2, final_gamma.reshape(1, d), table]
    out_specs, out_shape = [row], [jax.ShapeDtypeStruct((t, d), F32)]
    if next_proj is not None:
        p_gamma, p_shift, p_scale, p_w = next_proj
        n = p_w.shape[1]
        in_specs += [pl.BlockSpec((1, d), lambda i: (0, 0)), per_batch, per_batch,
                     pl.BlockSpec((d, n), lambda i: (0, 0), pipeline_mode=pl.Buffered(1))]
        args += [p_gamma.reshape(1, d), p_shift, p_scale, p_w]
        out_specs.append(pl.BlockSpec((tb, n), lambda i: (i, 0)))
        out_shape.append(jax.ShapeDtypeStruct((t, n), BF16))
    res = pl.pallas_call(
        functools.partial(_peer_kernel, final_norm, next_proj is not None),
        grid=(t // tb,),
        in_specs=in_specs,
        out_specs=out_specs,
        out_shape=out_shape,
        scratch_shapes=[pltpu.VMEM((PEER_SLOTS, ne, d), jnp.uint32),
                        pltpu.VMEM((PEER_SLOTS, d), F32),
                        pltpu.SemaphoreType.DMA((PEER_SLOTS,))],
        compiler_params=_params(("arbitrary",)),
        name="peer_experts",
    )(*args)
    return res if next_proj is not None else res[0]


def _peer_layer(x, mod, gamma, w_q, sub_keys, u_all, v_all, layer, seq, final_gamma, final_norm, next_proj=None):
    bsz, d = mod.shape[0], x.shape[1]
    sh2 = mod[:, 3 * d:4 * d].reshape(bsz, 1, d)
    sc2 = mod[:, 4 * d:5 * d].reshape(bsz, 1, d)
    g2 = mod[:, 5 * d:6 * d].reshape(bsz, 1, d)
    q, h = _norm_mod_matmul(x, gamma, sh2, sc2, w_q.astype(BF16), seq, emit_h=True)
    ids, gates = _peer_route(q, sub_keys)
    table = _pack_experts(u_all, v_all, layer)
    return _peer_experts(ids.T, gates, h, x, g2, table, seq, final_gamma, final_norm, next_proj)


def kernel(x, c, ctx, c_ctx, w_mod, b_mod, norm_mix_g, norm_ffn_g, norm_final_g, lru_w_in, lru_conv_w, lru_conv_b, lru_w_a, lru_b_a, lru_w_x, lru_b_x, lru_lambda, lru_w_out, sc_w_in, sc_conv_w, sc_conv_b, sc_w_out, peer_w_q, peer_sub_keys, peer_u, peer_v):
    bsz, seq, d = x.shape
    ctx_len = ctx.shape[1]
    xt = x.reshape(bsz * seq, d)

    c_rows = jnp.concatenate([c, c_ctx[None, :], jnp.zeros((SUBLANES - bsz - 1, d), F32)], axis=0)
    mods = [_mod_vectors(c_rows, w_mod, b_mod, i) for i in range(w_mod.shape[0])]

    mod = mods[0][:bsz]
    sh1, sc1, g1 = (mod[:, k * d:(k + 1) * d].reshape(bsz, 1, d) for k in range(3))
    w_in16 = lru_w_in[0].astype(BF16)
    w_a16, w_x16 = lru_w_a[0].astype(BF16), lru_w_x[0].astype(BF16)
    z = _norm_mod_matmul(xt, norm_mix_g[0], sh1, sc1, w_in16, seq, out_dtype=BF16)

    mod_c = mods[0][bsz:bsz + 1]
    sh_c = jnp.broadcast_to(mod_c[:, 0:d].reshape(1, 1, d), (bsz, 1, d))
    sc_c = jnp.broadcast_to(mod_c[:, d:2 * d].reshape(1, 1, d), (bsz, 1, d))
    z_ctx = _norm_mod_matmul(ctx.reshape(bsz * ctx_len, d), norm_mix_g[0], sh_c, sc_c, w_in16, ctx_len,
                             out_dtype=BF16)
    coef_args = (1, lru_conv_w[0], lru_conv_b[0], w_a16, lru_b_a[0], w_x16, lru_b_x[0], lru_lambda[0])
    cf = [t.reshape(bsz, ctx_len, d) for t in _lru_coeffs(z_ctx, *coef_args, ctx_len, ctx_len)]
    zero_state = jnp.zeros((bsz, 1, d), F32)
    hc_f = _linear_scan(cf[0], cf[1], zero_state, False)
    hc_r = _linear_scan(cf[2], cf[3], zero_state, True)

    lf = [t.reshape(bsz, seq, d) for t in _lru_coeffs(z, *coef_args, GRID_W, 256)]
    h_f = _linear_scan(lf[0], lf[1], hc_f[:, ctx_len - 1:ctx_len, :], False)
    y = _linear_scan(lf[2], lf[3], hc_r[:, 0:1, :], True, prev=h_f)
    xt = _lru_out(z, y.reshape(bsz * seq, d), xt, g1, lru_w_out[0].astype(BF16), seq)
    mod1 = mods[1][:bsz]
    sh1, sc1, g1 = (mod1[:, k * d:(k + 1) * d].reshape(bsz, 1, d) for k in range(3))
    xt, z = _peer_layer(xt, mod, norm_ffn_g[0], peer_w_q[0], peer_sub_keys[0], peer_u, peer_v, 0, seq,
                        norm_final_g, False, next_proj=(norm_mix_g[1], sh1, sc1, sc_w_in[0].astype(BF16)))
    mod = mod1
    xt = _sc_out(z, sc_conv_w[0], sc_conv_b[0], xt, g1, sc_w_out[0].astype(BF16), seq)
    xt = _peer_layer(xt, mod, norm_ffn_g[1], peer_w_q[1], peer_sub_keys[1], peer_u, peer_v, 1, seq,
                     norm_final_g, True)
    return xt.reshape(bsz, seq, d)
```

```python
import functools

import jax
import jax.numpy as jnp
from jax import lax
from jax.experimental import pallas as pl
from jax.experimental.pallas import tpu as pltpu

EPS = 1e-6
GRID_W = 64
LRU_HEADS = 16
LRU_C = 8.0
PEER_HEADS = 8
PEER_N_KEYS = 128
PEER_TOPK = 16
PEER_PER_TOKEN = PEER_HEADS * PEER_TOPK

LANES = 128
SUBLANES = 8
VMEM_LIMIT = 52 * 1024 * 1024

F32 = jnp.float32
BF16 = jnp.bfloat16
HIGHEST = lax.Precision.HIGHEST


def _params(sem):
    return pltpu.CompilerParams(dimension_semantics=sem, vmem_limit_bytes=VMEM_LIMIT)


def _gelu_tanh(x):
    c = 0.7978845608028654
    return x * (0.5 * (1.0 + jnp.tanh(c * (x + 0.044715 * (x * x * x)))))


def _neg_expm1_2x(half_z, exp_z):
    return -jnp.tanh(half_z) * (exp_z + 1.0)


def _sigmoid(x):
    return 0.5 * (jnp.tanh(0.5 * x) + 1.0)


def _mod_kernel(c_ref, w_ref, b_ref, o_ref):
    c = c_ref[...]
    s = c * jax.nn.sigmoid(c)
    o_ref[...] = jnp.dot(s, w_ref[...], preferred_element_type=F32, precision=HIGHEST) + b_ref[...]


def _mod_vectors(c_rows, w_all, b_all, layer):
    nl, d, n = w_all.shape
    tn = 1024
    return pl.pallas_call(
        _mod_kernel,
        grid=(n // tn,),
        in_specs=[pl.BlockSpec((SUBLANES, d), lambda j: (0, 0)),
                  pl.BlockSpec((None, d, tn), lambda j: (layer, 0, j)),
                  pl.BlockSpec((None, 1, tn), lambda j: (layer, 0, j))],
        out_specs=pl.BlockSpec((SUBLANES, tn), lambda j: (0, j)),
        out_shape=jax.ShapeDtypeStruct((SUBLANES, n), F32),
        compiler_params=_params(("arbitrary",)),
        name="mod_vectors",
    )(c_rows, w_all, b_all.reshape(nl, 1, n))


def _nmm_kernel(emit_h, x_ref, g_ref, sh_ref, sc_ref, w_ref, o_ref, *rest):
    if emit_h:
        h_ref, hn_ref = rest
    else:
        (hn_ref,) = rest

    @pl.when(pl.program_id(1) == 0)
    def _():
        x = x_ref[...]
        ms = jnp.mean(x * x, axis=-1, keepdims=True)
        y = (x * lax.rsqrt(ms + EPS)) * g_ref[...]
        h = y * (1.0 + sc_ref[0]) + sh_ref[0]
        hn_ref[...] = h.astype(BF16)
        if emit_h:
            h_ref[...] = h

    o_ref[...] = jnp.dot(hn_ref[...], w_ref[...], preferred_element_type=F32).astype(o_ref.dtype)


def _norm_mod_matmul(x, gamma, shift, scale, w_bf16, rows_per_batch, emit_h=False, out_dtype=F32):
    t, d = x.shape
    n = w_bf16.shape[1]
    tn = 1024
    tm = min(512 if emit_h else 1024, rows_per_batch)
    bpb = rows_per_batch // tm
    out_shape = [jax.ShapeDtypeStruct((t, n), out_dtype)]
    out_specs = [pl.BlockSpec((tm, tn), lambda i, j: (i, j))]
    if emit_h:
        out_shape.append(jax.ShapeDtypeStruct((t, d), F32))
        out_specs.append(pl.BlockSpec((tm, d), lambda i, j: (i, 0)))
    res = pl.pallas_call(
        functools.partial(_nmm_kernel, emit_h),
        grid=(t // tm, n // tn),
        in_specs=[pl.BlockSpec((tm, d), lambda i, j: (i, 0)),
                  pl.BlockSpec((1, d), lambda i, j: (0, 0)),
                  pl.BlockSpec((1, 1, d), lambda i, j: (i // bpb, 0, 0)),
                  pl.BlockSpec((1, 1, d), lambda i, j: (i // bpb, 0, 0)),
                  pl.BlockSpec((d, tn), lambda i, j: (0, j))],
        out_specs=out_specs,
        out_shape=out_shape,
        scratch_shapes=[pltpu.VMEM((tm, d), BF16)],
        compiler_params=_params(("arbitrary", "arbitrary")),
        name="norm_mod_matmul",
    )(x, gamma.reshape(1, d), shift, scale, w_bf16)
    return res if emit_h else res[0]


def _row_conv(u, taps, bias, pad_left, row_len):
    tm = u.shape[0]
    pos = lax.broadcasted_iota(jnp.int32, (tm, 1), 0) % row_len
    y = None
    for k in range(len(taps)):
        off = k - pad_left
        if off == 0:
            term = u
        else:
            rolled = pltpu.roll(u, (-off) % tm, axis=0)
            valid = jnp.logical_and(pos + off >= 0, pos + off < row_len)
            term = jnp.where(valid, rolled, 0.0)
        y = (bias + taps[k] * term) if y is None else (y + taps[k] * term)
    return y


def _lru_coef_kernel(row_len, xb_ref, cw_ref, cb_ref, wa_ref, ba_ref, wx_ref, bx_ref, lam_ref,
                     af_ref, bf_ref, ar_ref, br_ref):
    taps = [cw_ref[k:k + 1, :] for k in range(cw_ref.shape[0])]
    xc = _row_conv(xb_ref[...].astype(F32), taps, cb_ref[...], 2, row_len)
    d = xc.shape[1]
    hd = d // LRU_HEADS
    outs = ((af_ref, bf_ref), (ar_ref, br_ref))
    for dr in range(2):
        lam = lam_ref[dr:dr + 1, :]
        z = -lam
        nsp = (-LRU_C) * (jnp.maximum(z, 0.0) + jnp.log1p(jnp.exp(-jnp.abs(z))))
        a_ref, b_ref = outs[dr]
        for h in range(LRU_HEADS):
            sl = slice(h * hd, (h + 1) * hd)
            xh = xc[:, sl]
            xh16 = xh.astype(BF16)
            r = _sigmoid(jnp.dot(xh16, wa_ref[dr, h], preferred_element_type=F32) + ba_ref[dr:dr + 1, sl])
            i = _sigmoid(jnp.dot(xh16, wx_ref[dr, h], preferred_element_type=F32) + bx_ref[dr:dr + 1, sl])
            log_a = r * nsp[:, sl]
            a = jnp.exp(log_a)
            a_ref[:, sl] = a
            b_ref[:, sl] = jnp.sqrt(_neg_expm1_2x(log_a, a * a)) * (i * xh)


def _lru_coeffs(z, col_block, conv_w, conv_b, w_a16, b_a, w_x16, b_x, lam, row_len, tm):
    t = z.shape[0]
    d = conv_w.shape[1]
    full = lambda *s: pl.BlockSpec(s, lambda i: (0,) * len(s))
    o_spec = pl.BlockSpec((tm, d), lambda i: (i, 0))
    o_shape = jax.ShapeDtypeStruct((t, d), F32)
    return pl.pallas_call(
        functools.partial(_lru_coef_kernel, row_len),
        grid=(t // tm,),
        in_specs=[pl.BlockSpec((tm, d), lambda i: (i, col_block)),
                  full(*conv_w.shape), full(1, d),
                  full(*w_a16.shape), full(2, d), full(*w_x16.shape), full(2, d), full(2, d)],
        out_specs=[o_spec] * 4,
        out_shape=[o_shape] * 4,
        compiler_params=_params(("arbitrary",)),
        name="lru_coeffs",
    )(z, conv_w, conv_b.reshape(1, d), w_a16, b_a, w_x16, b_x, lam)


def _scan_kernel(reverse, add_prev, a_ref, b_ref, h0_ref, *rest):
    if add_prev:
        y_ref, o_ref, st_ref = rest
    else:
        o_ref, st_ref = rest
    ts, cw = a_ref.shape
    ng = ts // SUBLANES

    @pl.when(pl.program_id(2) == 0)
    def _():
        st_ref[...] = jnp.broadcast_to(h0_ref[...], st_ref.shape)

    row = lax.broadcasted_iota(jnp.int32, (SUBLANES, cw), 0)

    def body(gi, h):
        g = (ng - 1 - gi) if reverse else gi
        off = pl.multiple_of(g * SUBLANES, SUBLANES)
        a = a_ref[pl.ds(off, SUBLANES), :]
        b = b_ref[pl.ds(off, SUBLANES), :]
        for s in (1, 2, 4):
            if reverse:
                shift, m = SUBLANES - s, row < SUBLANES - s
            else:
                shift, m = s, row >= s
            a_s = jnp.where(m, pltpu.roll(a, shift, axis=0), 1.0)
            b_s = jnp.where(m, pltpu.roll(b, shift, axis=0), 0.0)
            b = a * b_s + b
            a = a * a_s
        hh = a * h + b
        if add_prev:
            o_ref[pl.ds(off, SUBLANES), :] = hh + y_ref[pl.ds(off, SUBLANES), :]
        else:
            o_ref[pl.ds(off, SUBLANES), :] = hh
        last = hh[0:1, :] if reverse else hh[SUBLANES - 1:SUBLANES, :]
        return jnp.broadcast_to(last, (SUBLANES, cw))

    st_ref[...] = lax.fori_loop(0, ng, body, st_ref[...], unroll=2)


def _linear_scan(a, b, h0, reverse, prev=None, ts=1024, cw=512):
    bsz, s, c = a.shape
    ts = min(ts, s)
    ns = s // ts
    tmap = (lambda bb, cc, j: (bb, ns - 1 - j, cc)) if reverse else (lambda bb, cc, j: (bb, j, cc))
    blk = pl.BlockSpec((None, ts, cw), tmap)
    ins = [a, b, h0]
    in_specs = [blk, blk, pl.BlockSpec((None, 1, cw), lambda bb, cc, j: (bb, 0, cc))]
    if prev is not None:
        ins.append(prev)
        in_specs.append(blk)
    return pl.pallas_call(
        functools.partial(_scan_kernel, reverse, prev is not None),
        grid=(bsz, c // cw, ns),
        in_specs=in_specs,
        out_specs=blk,
        out_shape=jax.ShapeDtypeStruct((bsz, s, c), F32),
        scratch_shapes=[pltpu.VMEM((SUBLANES, cw), F32)],
        compiler_params=_params(("arbitrary", "arbitrary", "arbitrary")),
        name="linear_scan",
    )(*ins)


def _lru_out_kernel(gate_ref, y_ref, x_ref, g1_ref, w_ref, o_ref):
    p = _gelu_tanh(gate_ref[...].astype(F32)) * y_ref[...]
    out = jnp.dot(p.astype(BF16), w_ref[...], preferred_element_type=F32)
    o_ref[...] = x_ref[...] + g1_ref[0] * out


def _lru_out(z, y, x, g1, w16, rows_per_batch, tm=512):
    t, d = x.shape
    bpb = rows_per_batch // tm
    row = pl.BlockSpec((tm, d), lambda i: (i, 0))
    return pl.pallas_call(
        _lru_out_kernel,
        grid=(t // tm,),
        in_specs=[row, row, row,
                  pl.BlockSpec((1, 1, d), lambda i: (i // bpb, 0, 0)),
                  pl.BlockSpec((d, d), lambda i: (0, 0), pipeline_mode=pl.Buffered(1))],
        out_specs=row,
        out_shape=jax.ShapeDtypeStruct((t, d), F32),
        compiler_params=_params(("arbitrary",)),
        name="lru_out",
    )(z, y, x, g1, w16)


def _sc_out_kernel(z_ref, cw_ref, cb_ref, x_ref, g1_ref, w_ref, o_ref):
    d = x_ref.shape[1]
    z = jnp.concatenate([z_ref[k] for k in range(z_ref.shape[0])], axis=1).astype(F32)
    bg, cg, v = z[:, :d], z[:, d:2 * d], z[:, 2 * d:]
    taps = [cw_ref[k:k + 1, :] for k in range(cw_ref.shape[0])]
    y = _row_conv(cg * v, taps, cb_ref[...], 1, GRID_W)
    p = bg * y
    out = jnp.dot(p.astype(BF16), w_ref[...], preferred_element_type=F32)
    o_ref[...] = x_ref[...] + g1_ref[0] * out


def _sc_out(z, conv_w, conv_b, x, g1, w16, rows_per_batch, tm=512):
    t, d = x.shape
    bpb = rows_per_batch // tm
    row = pl.BlockSpec((tm, d), lambda i: (i, 0))
    return pl.pallas_call(
        _sc_out_kernel,
        grid=(t // tm,),
        in_specs=[pl.BlockSpec((z.shape[0], tm, z.shape[2]), lambda i: (0, i, 0)),
                  pl.BlockSpec(conv_w.shape, lambda i: (0, 0)),
                  pl.BlockSpec((1, d), lambda i: (0, 0)),
                  row,
                  pl.BlockSpec((1, 1, d), lambda i: (i // bpb, 0, 0)),
                  pl.BlockSpec((d, d), lambda i: (0, 0), pipeline_mode=pl.Buffered(1))],
        out_specs=row,
        out_shape=jax.ShapeDtypeStruct((t, d), F32),
        compiler_params=_params(("arbitrary",)),
        name="sc_out",
    )(z, conv_w, conv_b.reshape(1, d), x, g1, w16)


def _top_rows(vals, idx_f, fill, n_out, emit):
    for r in range(n_out):
        m = jnp.max(vals, axis=0, keepdims=True)
        am = jnp.min(jnp.where(vals == m, idx_f, fill), axis=0, keepdims=True)
        emit(r, m, am)
        if r + 1 < n_out:
            vals = jnp.where(idx_f == am, -jnp.inf, vals)


def _route_kernel(q_ref, k_ref, ids_ref, gate_ref, sv_ref, si_ref, cv_ref, cand_ref, eid_ref, flat_ref):
    nk, dk = k_ref.shape[1], k_ref.shape[2]
    kk = PEER_TOPK
    n_cand = cand_ref.shape[0]
    tb = q_ref.shape[0]
    key_idx = lax.broadcasted_iota(jnp.int32, (nk, tb), 0).astype(F32)
    for p in range(2):
        s_t = lax.dot_general(k_ref[p], q_ref[:, p * dk:(p + 1) * dk], (((1,), (1,)), ((), ())),
                              precision=HIGHEST, preferred_element_type=F32)

        def emit(r, m, am, p=p):
            sv_ref[p, r:r + 1, :] = m
            si_ref[p, r:r + 1, :] = am

        _top_rows(s_t, key_idx, float(nk), kk, emit)

    sv1, si1 = sv_ref[1], si_ref[1]
    off = 0
    for i in range(kk):
        n_i = kk // (i + 1)
        cand_ref[off:off + n_i, :] = sv_ref[0, i:i + 1, :] + sv1[0:n_i, :]
        eid_ref[off:off + n_i, :] = si_ref[0, i:i + 1, :] * float(nk) + si1[0:n_i, :]
        flat_ref[off:off + n_i, :] = lax.broadcasted_iota(jnp.int32, (n_i, tb), 0).astype(F32) + float(i * kk)
        off += n_i
    cand_ref[off:n_cand, :] = jnp.full((n_cand - off, tb), -jnp.inf, F32)
    eid_ref[off:n_cand, :] = jnp.zeros((n_cand - off, tb), F32)
    flat_ref[off:n_cand, :] = jnp.full((n_cand - off, tb), float(kk * kk), F32)
    eid, flat = eid_ref[...], flat_ref[...]

    def emit2(r, m, am):
        e = jnp.max(jnp.where(flat == am, eid, -1.0), axis=0, keepdims=True)
        cv_ref[r:r + 1, :] = m
        ids_ref[r:r + 1, :] = e.astype(jnp.int32)

    _top_rows(cand_ref[...], flat, float(kk * kk), kk, emit2)
    cv = cv_ref[...]
    ex = jnp.exp(cv - jnp.max(cv, axis=0, keepdims=True))
    gate_ref[...] = ex / jnp.sum(ex, axis=0, keepdims=True)


def _peer_route(q, sub_keys, tb=1024):
    t = q.shape[0]
    nh, _, nk, dk = sub_keys.shape
    kk = PEER_TOPK
    n_cand = sum(kk // (i + 1) for i in range(kk))
    n_cand += (-n_cand) % SUBLANES
    o_spec = pl.BlockSpec((kk, tb), lambda i, h: (h, i))
    return pl.pallas_call(
        _route_kernel,
        grid=(t // tb, nh),
        in_specs=[pl.BlockSpec((tb, 2 * dk), lambda i, h: (i, h)),
                  pl.BlockSpec((None, 2, nk, dk), lambda i, h: (h, 0, 0, 0))],
        out_specs=[o_spec, o_spec],
        out_shape=[jax.ShapeDtypeStruct((nh * kk, t), jnp.int32),
                   jax.ShapeDtypeStruct((nh * kk, t), F32)],
        scratch_shapes=[pltpu.VMEM((2, kk, tb), F32), pltpu.VMEM((2, kk, tb), F32), pltpu.VMEM((kk, tb), F32)]
        + [pltpu.VMEM((n_cand, tb), F32)] * 3,
        compiler_params=_params(("arbitrary", "arbitrary")),
        name="peer_route",
    )(q, sub_keys)


def _pack_pair(x):
    half = x.shape[1] // 2
    bits = pltpu.bitcast(x.astype(BF16).astype(F32), jnp.uint32)
    return bits[:, half:] | (bits[:, :half] >> 16)


def _unpack_pair(w):
    return pltpu.bitcast(w << 16, F32), pltpu.bitcast(w & jnp.uint32(0xFFFF0000), F32)


def _pack_kernel(u_ref, v_ref, o_ref):
    half = u_ref.shape[1] // 2
    o_ref[:, 0, :half] = _pack_pair(u_ref[...])
    o_ref[:, 0, half:] = _pack_pair(v_ref[...])


def _pack_experts(u_all, v_all, layer, te=512):
    _, e, d = u_all.shape
    src = pl.BlockSpec((None, te, d), lambda i: (layer, i, 0))
    return pl.pallas_call(
        _pack_kernel,
        grid=(e // te,),
        in_specs=[src, src],
        out_specs=pl.BlockSpec((te, 1, d), lambda i: (i, 0, 0)),
        out_shape=jax.ShapeDtypeStruct((e, 1, d), jnp.uint32),
        compiler_params=_params(("arbitrary",)),
        name="pack_experts",
    )(u_all, v_all)


PEER_TOKENS_PER_STEP = 128
PEER_SLOTS = SUBLANES


def _peer_kernel(final_norm, project, head_ref, ids_ref, gate_ref, h_ref, x_ref, g2_ref, fin_ref, tab_ref, *rest):
    if project:
        pg_ref, psh_ref, psc_ref, pw_ref, o_ref, z_ref, buf_ref, orow_ref, hprev_ref, sem_ref = rest
    else:
        o_ref, buf_ref, orow_ref, sem_ref = rest
    tb, d = h_ref.shape
    ne = PEER_PER_TOKEN
    nslots = buf_ref.shape[0]
    nword = d // (2 * LANES)
    ahead = nslots - 1
    step = pl.program_id(0)
    last_step = pl.num_programs(0) - (2 if project else 1)

    def project_piece(g, part=None):
        cols = slice(None) if part is None else slice(part * LANES, (part + 1) * LANES)
        z_ref[g, :, cols] = jnp.dot(hprev_ref[...], pw_ref[g, :, cols],
                                    preferred_element_type=F32).astype(z_ref.dtype)

    def start_rows(src_ids, tok, slot, part=0, nparts=1):
        for k in range(part * ne // nparts, (part + 1) * ne // nparts):
            pltpu.make_async_copy(tab_ref.at[src_ids[tok, k]], buf_ref.at[slot, pl.ds(k, 1), :],
                                  sem_ref.at[slot]).start(priority=k % 2)

    def wait_token(slot):
        pltpu.make_async_copy(tab_ref.at[pl.ds(0, ne), 0], buf_ref.at[slot], sem_ref.at[slot]).wait()

    lane_tok = lax.broadcasted_iota(jnp.int32, (ne, tb), 1)

    def token(base, j, issue_next, side_work=None):
        wait_token(j)
        if side_work is not None:
            side_work()
        rows = pl.ds(base, nslots)
        acc = jnp.zeros((ne, LANES), F32)
        for c in range(nword):
            lo, hi = _unpack_pair(buf_ref[j, :, c * LANES:(c + 1) * LANES])
            acc = acc + lo * h_ref[rows, c * LANES:(c + 1) * LANES][j:j + 1, :]
            acc = acc + hi * h_ref[rows, (nword + c) * LANES:(nword + c + 1) * LANES][j:j + 1, :]
            issue_next(c, 2 * nword)
        act = _gelu_tanh(jnp.sum(acc, axis=1, keepdims=True))
        gcol = jnp.sum(jnp.where(lane_tok == base + j, gate_ref[...], 0.0), axis=1, keepdims=True)
        wb = jnp.broadcast_to(gcol * act, (ne, LANES))
        for c in range(nword):
            lo, hi = _unpack_pair(buf_ref[j, :, (nword + c) * LANES:(nword + c + 1) * LANES])
            orow_ref[j:j + 1, c * LANES:(c + 1) * LANES] = jnp.sum(lo * wb, axis=0, keepdims=True)
            orow_ref[j:j + 1, (nword + c) * LANES:(nword + c + 1) * LANES] = jnp.sum(hi * wb, axis=0, keepdims=True)
            issue_next(nword + c, 2 * nword)

    def finish_group(base):
        rows = pl.ds(base, nslots)
        y = x_ref[rows, :] + g2_ref[0] * orow_ref[...]
        if final_norm:
            y = (y * lax.rsqrt(jnp.mean(y * y, axis=-1, keepdims=True) + EPS)) * fin_ref[...]
        o_ref[rows, :] = y

    ngroups = tb // nslots

    def group(g, carry):
        base = pl.multiple_of(g * nslots, nslots)
        nparts = pw_ref.shape[2] // LANES if project else 0
        for j in range(nslots):
            token(base, j, functools.partial(start_rows, ids_ref, base + j, (j + ahead) % nslots),
                  functools.partial(project_piece, g, j) if j < nparts else None)
        finish_group(base)
        return carry

    def consume_block():
        @pl.when(step == 0)
        def _():
            def first(s, carry):
                start_rows(head_ref, s, s)
                return carry
            lax.fori_loop(0, ahead, first, 0)

        lax.fori_loop(0, ngroups, group, 0)

        @pl.when(step == last_step)
        def _():
            for s in range(ahead):
                wait_token(s)

    if not project:
        consume_block()
        return

    @pl.when(step == 0)
    def _():
        hprev_ref[...] = jnp.zeros(hprev_ref.shape, hprev_ref.dtype)

    @pl.when(step <= last_step)
    def _():
        consume_block()
        y = o_ref[...]
        hn = (y * lax.rsqrt(jnp.mean(y * y, axis=-1, keepdims=True) + EPS)) * pg_ref[...]
        hprev_ref[...] = (hn * (1.0 + psc_ref[0]) + psh_ref[0]).astype(hprev_ref.dtype)

    @pl.when(step > last_step)
    def _():
        for g in range(ngroups):
            project_piece(g)


def _peer_experts(ids_t, gates, h, x, g2, table, rows_per_batch, final_gamma, final_norm, next_proj=None):
    t, d = x.shape
    tb = PEER_TOKENS_PER_STEP
    ne = PEER_PER_TOKEN
    bpb = rows_per_batch // tb
    ahead = PEER_SLOTS - 1
    ids_ahead = jnp.concatenate([ids_t[ahead:], jnp.zeros((ahead, ne), jnp.int32)], axis=0)
    nblocks = t // tb
    nsteps = nblocks
    blk = lambda i: jnp.minimum(i, nblocks - 1)
    row = pl.BlockSpec((tb, d), lambda i: (blk(i), 0))
    per_batch = pl.BlockSpec((1, 1, d), lambda i: (blk(i) // bpb, 0, 0))
    in_specs = [pl.BlockSpec((PEER_SLOTS, ne), lambda i: (0, 0), memory_space=pltpu.SMEM),
                pl.BlockSpec((tb, ne), lambda i: (blk(i), 0), memory_space=pltpu.SMEM),
                pl.BlockSpec((ne, tb), lambda i: (0, blk(i))),
                row, row, per_batch,
                pl.BlockSpec((1, d), lambda i: (0, 0)),
                pl.BlockSpec(memory_space=pl.ANY)]
    args = [ids_t[:PEER_SLOTS], ids_ahead, gates, h, x, g2, final_gamma.reshape(1, d), table]
    out_specs, out_shape = [row], [jax.ShapeDtypeStruct((t, d), F32)]
    scratch = [pltpu.VMEM((PEER_SLOTS, ne, d), jnp.uint32), pltpu.VMEM((PEER_SLOTS, d), F32)]
    if next_proj is not None:
        p_gamma, p_shift, p_scale, p_w = next_proj
        ng = tb // PEER_SLOTS
        nc = p_w.shape[1] // ng
        in_specs += [pl.BlockSpec((1, d), lambda i: (0, 0)), per_batch, per_batch,
                     pl.BlockSpec((ng, d, nc), lambda i: (0, 0, 0), pipeline_mode=pl.Buffered(1))]
        args += [p_gamma.reshape(1, d), p_shift, p_scale, p_w.reshape(d, ng, nc).transpose(1, 0, 2)]
        out_specs.append(pl.BlockSpec((ng, tb, nc), lambda i: (0, jnp.maximum(i - 1, 0), 0)))
        out_shape.append(jax.ShapeDtypeStruct((ng, t, nc), BF16))
        scratch.append(pltpu.VMEM((tb, d), BF16))
        nsteps += 1
    res = pl.pallas_call(
        functools.partial(_peer_kernel, final_norm, next_proj is not None),
        grid=(nsteps,),
        in_specs=in_specs,
        out_specs=out_specs,
        out_shape=out_shape,
        scratch_shapes=scratch + [pltpu.SemaphoreType.DMA((PEER_SLOTS,))],
        compiler_params=_params(("arbitrary",)),
        name="peer_experts",
    )(*args)
    return res if next_proj is not None else res[0]


def _peer_layer(x, mod, gamma, w_q, sub_keys, u_all, v_all, layer, seq, final_gamma, final_norm, next_proj=None):
    bsz, d = mod.shape[0], x.shape[1]
    sh2 = mod[:, 3 * d:4 * d].reshape(bsz, 1, d)
    sc2 = mod[:, 4 * d:5 * d].reshape(bsz, 1, d)
    g2 = mod[:, 5 * d:6 * d].reshape(bsz, 1, d)
    q, h = _norm_mod_matmul(x, gamma, sh2, sc2, w_q.astype(BF16), seq, emit_h=True)
    ids, gates = _peer_route(q, sub_keys)
    table = _pack_experts(u_all, v_all, layer)
    return _peer_experts(ids.T, gates, h, x, g2, table, seq, final_gamma, final_norm, next_proj)


def kernel(x, c, ctx, c_ctx, w_mod, b_mod, norm_mix_g, norm_ffn_g, norm_final_g, lru_w_in, lru_conv_w, lru_conv_b, lru_w_a, lru_b_a, lru_w_x, lru_b_x, lru_lambda, lru_w_out, sc_w_in, sc_conv_w, sc_conv_b, sc_w_out, peer_w_q, peer_sub_keys, peer_u, peer_v):
    bsz, seq, d = x.shape
    ctx_len = ctx.shape[1]
    xt = x.reshape(bsz * seq, d)

    c_rows = jnp.concatenate([c, c_ctx[None, :], jnp.zeros((SUBLANES - bsz - 1, d), F32)], axis=0)
    mods = [_mod_vectors(c_rows, w_mod, b_mod, i) for i in range(w_mod.shape[0])]

    mod = mods[0][:bsz]
    sh1, sc1, g1 = (mod[:, k * d:(k + 1) * d].reshape(bsz, 1, d) for k in range(3))
    w_in16 = lru_w_in[0].astype(BF16)
    w_a16, w_x16 = lru_w_a[0].astype(BF16), lru_w_x[0].astype(BF16)
    z = _norm_mod_matmul(xt, norm_mix_g[0], sh1, sc1, w_in16, seq, out_dtype=BF16)

    mod_c = mods[0][bsz:bsz + 1]
    sh_c = jnp.broadcast_to(mod_c[:, 0:d].reshape(1, 1, d), (bsz, 1, d))
    sc_c = jnp.broadcast_to(mod_c[:, d:2 * d].reshape(1, 1, d), (bsz, 1, d))
    z_ctx = _norm_mod_matmul(ctx.reshape(bsz * ctx_len, d), norm_mix_g[0], sh_c, sc_c, w_in16, ctx_len,
                             out_dtype=BF16)
    coef_args = (1, lru_conv_w[0], lru_conv_b[0], w_a16, lru_b_a[0], w_x16, lru_b_x[0], lru_lambda[0])
    cf = [t.reshape(bsz, ctx_len, d) for t in _lru_coeffs(z_ctx, *coef_args, ctx_len, ctx_len)]
    zero_state = jnp.zeros((bsz, 1, d), F32)
    hc_f = _linear_scan(cf[0], cf[1], zero_state, False)
    hc_r = _linear_scan(cf[2], cf[3], zero_state, True)

    lf = [t.reshape(bsz, seq, d) for t in _lru_coeffs(z, *coef_args, GRID_W, 256)]
    h_f = _linear_scan(lf[0], lf[1], hc_f[:, ctx_len - 1:ctx_len, :], False)
    y = _linear_scan(lf[2], lf[3], hc_r[:, 0:1, :], True, prev=h_f)
    xt = _lru_out(z, y.reshape(bsz * seq, d), xt, g1, lru_w_out[0].astype(BF16), seq)
    mod1 = mods[1][:bsz]
    sh1, sc1, g1 = (mod1[:, k * d:(k + 1) * d].reshape(bsz, 1, d) for k in range(3))
    xt, z = _peer_layer(xt, mod, norm_ffn_g[0], peer_w_q[0], peer_sub_keys[0], peer_u, peer_v, 0, seq,
                        norm_final_g, False, next_proj=(norm_mix_g[1], sh1, sc1, sc_w_in[0].astype(BF16)))
    mod = mod1
    xt = _sc_out(z, sc_conv_w[0], sc_conv_b[0], xt, g1, sc_w_out[0].astype(BF16), seq)
    xt = _peer_layer(xt, mod, norm_ffn_g[1], peer_w_q[1], peer_sub_keys[1], peer_u, peer_v, 1, seq,
                     norm_final_g, True)
    return xt.reshape(bsz, seq, d)
```

```python
import functools

import jax
import jax.numpy as jnp
from jax import lax
from jax.experimental import pallas as pl
from jax.experimental.pallas import tpu as pltpu

EPS = 1e-6
GRID_W = 64
LRU_HEADS = 16
LRU_C = 8.0
PEER_HEADS = 8
PEER_N_KEYS = 128
PEER_TOPK = 16
PEER_PER_TOKEN = PEER_HEADS * PEER_TOPK

LANES = 128
SUBLANES = 8
VMEM_LIMIT = 48 * 1024 * 1024

F32 = jnp.float32
BF16 = jnp.bfloat16
HIGHEST = lax.Precision.HIGHEST


def _params(sem):
    return pltpu.CompilerParams(dimension_semantics=sem, vmem_limit_bytes=VMEM_LIMIT)


def _gelu_tanh(x):
    c = 0.7978845608028654
    return x * (0.5 * (1.0 + jnp.tanh(c * (x + 0.044715 * (x * x * x)))))


def _neg_expm1_2x(half_z, exp_z):
    return -jnp.tanh(half_z) * (exp_z + 1.0)


def _sigmoid(x):
    return 0.5 * (jnp.tanh(0.5 * x) + 1.0)


def _mod_kernel(c_ref, w_ref, b_ref, o_ref):
    c = c_ref[...]
    s = c * jax.nn.sigmoid(c)
    o_ref[...] = jnp.dot(s, w_ref[...], preferred_element_type=F32, precision=HIGHEST) + b_ref[...]


def _mod_vectors(c_rows, w_all, b_all, layer):
    nl, d, n = w_all.shape
    tn = 1024
    return pl.pallas_call(
        _mod_kernel,
        grid=(n // tn,),
        in_specs=[pl.BlockSpec((SUBLANES, d), lambda j: (0, 0)),
                  pl.BlockSpec((None, d, tn), lambda j: (layer, 0, j)),
                  pl.BlockSpec((None, 1, tn), lambda j: (layer, 0, j))],
        out_specs=pl.BlockSpec((SUBLANES, tn), lambda j: (0, j)),
        out_shape=jax.ShapeDtypeStruct((SUBLANES, n), F32),
        compiler_params=_params(("arbitrary",)),
        name="mod_vectors",
    )(c_rows, w_all, b_all.reshape(nl, 1, n))


def _nmm_kernel(emit_h, x_ref, g_ref, sh_ref, sc_ref, w_ref, o_ref, *rest):
    if emit_h:
        h_ref, hn_ref = rest
    else:
        (hn_ref,) = rest

    @pl.when(pl.program_id(1) == 0)
    def _():
        x = x_ref[...]
        ms = jnp.mean(x * x, axis=-1, keepdims=True)
        y = (x * lax.rsqrt(ms + EPS)) * g_ref[...]
        h = y * (1.0 + sc_ref[0]) + sh_ref[0]
        hn_ref[...] = h.astype(BF16)
        if emit_h:
            h_ref[...] = h

    o_ref[...] = jnp.dot(hn_ref[...], w_ref[...], preferred_element_type=F32).astype(o_ref.dtype)


def _norm_mod_matmul(x, gamma, shift, scale, w_bf16, rows_per_batch, emit_h=False, out_dtype=F32):
    t, d = x.shape
    n = w_bf16.shape[1]
    tn = 1024
    tm = min(512 if emit_h else 1024, rows_per_batch)
    bpb = rows_per_batch // tm
    out_shape = [jax.ShapeDtypeStruct((t, n), out_dtype)]
    out_specs = [pl.BlockSpec((tm, tn), lambda i, j: (i, j))]
    if emit_h:
        out_shape.append(jax.ShapeDtypeStruct((t, d), F32))
        out_specs.append(pl.BlockSpec((tm, d), lambda i, j: (i, 0)))
    res = pl.pallas_call(
        functools.partial(_nmm_kernel, emit_h),
        grid=(t // tm, n // tn),
        in_specs=[pl.BlockSpec((tm, d), lambda i, j: (i, 0)),
                  pl.BlockSpec((1, d), lambda i, j: (0, 0)),
                  pl.BlockSpec((1, 1, d), lambda i, j: (i // bpb, 0, 0)),
                  pl.BlockSpec((1, 1, d), lambda i, j: (i // bpb, 0, 0)),
                  pl.BlockSpec((d, tn), lambda i, j: (0, j))],
        out_specs=out_specs,
        out_shape=out_shape,
        scratch_shapes=[pltpu.VMEM((tm, d), BF16)],
        compiler_params=_params(("arbitrary", "arbitrary")),
        name="norm_mod_matmul",
    )(x, gamma.reshape(1, d), shift, scale, w_bf16)
    return res if emit_h else res[0]


def _row_conv(u, taps, bias, pad_left, row_len):
    tm = u.shape[0]
    pos = lax.broadcasted_iota(jnp.int32, (tm, 1), 0) % row_len
    y = None
    for k in range(len(taps)):
        off = k - pad_left
        if off == 0:
            term = u
        else:
            rolled = pltpu.roll(u, (-off) % tm, axis=0)
            valid = jnp.logical_and(pos + off >= 0, pos + off < row_len)
            term = jnp.where(valid, rolled, 0.0)
        y = (bias + taps[k] * term) if y is None else (y + taps[k] * term)
    return y


def _lru_coef_kernel(row_len, xb_ref, cw_ref, cb_ref, wa_ref, ba_ref, wx_ref, bx_ref, lam_ref,
                     af_ref, bf_ref, ar_ref, br_ref):
    taps = [cw_ref[k:k + 1, :] for k in range(cw_ref.shape[0])]
    xc = _row_conv(xb_ref[...].astype(F32), taps, cb_ref[...], 2, row_len)
    d = xc.shape[1]
    hd = d // LRU_HEADS
    outs = ((af_ref, bf_ref), (ar_ref, br_ref))
    for dr in range(2):
        lam = lam_ref[dr:dr + 1, :]
        z = -lam
        nsp = (-LRU_C) * (jnp.maximum(z, 0.0) + jnp.log1p(jnp.exp(-jnp.abs(z))))
        a_ref, b_ref = outs[dr]
        for h in range(LRU_HEADS):
            sl = slice(h * hd, (h + 1) * hd)
            xh = xc[:, sl]
            xh16 = xh.astype(BF16)
            r = _sigmoid(jnp.dot(xh16, wa_ref[dr, h], preferred_element_type=F32) + ba_ref[dr:dr + 1, sl])
            i = _sigmoid(jnp.dot(xh16, wx_ref[dr, h], preferred_element_type=F32) + bx_ref[dr:dr + 1, sl])
            log_a = r * nsp[:, sl]
            a = jnp.exp(log_a)
            a_ref[:, sl] = a
            b_ref[:, sl] = jnp.sqrt(_neg_expm1_2x(log_a, a * a)) * (i * xh)


def _lru_coeffs(z, col_block, conv_w, conv_b, w_a16, b_a, w_x16, b_x, lam, row_len, tm):
    t = z.shape[0]
    d = conv_w.shape[1]
    full = lambda *s: pl.BlockSpec(s, lambda i: (0,) * len(s))
    o_spec = pl.BlockSpec((tm, d), lambda i: (i, 0))
    o_shape = jax.ShapeDtypeStruct((t, d), F32)
    return pl.pallas_call(
        functools.partial(_lru_coef_kernel, row_len),
        grid=(t // tm,),
        in_specs=[pl.BlockSpec((tm, d), lambda i: (i, col_block)),
                  full(*conv_w.shape), full(1, d),
                  full(*w_a16.shape), full(2, d), full(*w_x16.shape), full(2, d), full(2, d)],
        out_specs=[o_spec] * 4,
        out_shape=[o_shape] * 4,
        compiler_params=_params(("arbitrary",)),
        name="lru_coeffs",
    )(z, conv_w, conv_b.reshape(1, d), w_a16, b_a, w_x16, b_x, lam)


def _scan_kernel(reverse, add_prev, a_ref, b_ref, h0_ref, *rest):
    if add_prev:
        y_ref, o_ref, st_ref = rest
    else:
        o_ref, st_ref = rest
    ts, cw = a_ref.shape
    ng = ts // SUBLANES

    @pl.when(pl.program_id(2) == 0)
    def _():
        st_ref[...] = jnp.broadcast_to(h0_ref[...], st_ref.shape)

    row = lax.broadcasted_iota(jnp.int32, (SUBLANES, cw), 0)

    def body(gi, h):
        g = (ng - 1 - gi) if reverse else gi
        off = pl.multiple_of(g * SUBLANES, SUBLANES)
        a = a_ref[pl.ds(off, SUBLANES), :]
        b = b_ref[pl.ds(off, SUBLANES), :]
        for s in (1, 2, 4):
            if reverse:
                shift, m = SUBLANES - s, row < SUBLANES - s
            else:
                shift, m = s, row >= s
            a_s = jnp.where(m, pltpu.roll(a, shift, axis=0), 1.0)
            b_s = jnp.where(m, pltpu.roll(b, shift, axis=0), 0.0)
            b = a * b_s + b
            a = a * a_s
        hh = a * h + b
        if add_prev:
            o_ref[pl.ds(off, SUBLANES), :] = hh + y_ref[pl.ds(off, SUBLANES), :]
        else:
            o_ref[pl.ds(off, SUBLANES), :] = hh
        last = hh[0:1, :] if reverse else hh[SUBLANES - 1:SUBLANES, :]
        return jnp.broadcast_to(last, (SUBLANES, cw))

    st_ref[...] = lax.fori_loop(0, ng, body, st_ref[...], unroll=2)


def _linear_scan(a, b, h0, reverse, prev=None, ts=1024, cw=512):
    bsz, s, c = a.shape
    ts = min(ts, s)
    ns = s // ts
    tmap = (lambda bb, cc, j: (bb, ns - 1 - j, cc)) if reverse else (lambda bb, cc, j: (bb, j, cc))
    blk = pl.BlockSpec((None, ts, cw), tmap)
    ins = [a, b, h0]
    in_specs = [blk, blk, pl.BlockSpec((None, 1, cw), lambda bb, cc, j: (bb, 0, cc))]
    if prev is not None:
        ins.append(prev)
        in_specs.append(blk)
    return pl.pallas_call(
        functools.partial(_scan_kernel, reverse, prev is not None),
        grid=(bsz, c // cw, ns),
        in_specs=in_specs,
        out_specs=blk,
        out_shape=jax.ShapeDtypeStruct((bsz, s, c), F32),
        scratch_shapes=[pltpu.VMEM((SUBLANES, cw), F32)],
        compiler_params=_params(("arbitrary", "arbitrary", "arbitrary")),
        name="linear_scan",
    )(*ins)


def _lru_out_kernel(gate_ref, y_ref, x_ref, g1_ref, w_ref, o_ref):
    p = _gelu_tanh(gate_ref[...].astype(F32)) * y_ref[...]
    out = jnp.dot(p.astype(BF16), w_ref[...], preferred_element_type=F32)
    o_ref[...] = x_ref[...] + g1_ref[0] * out


def _lru_out(z, y, x, g1, w16, rows_per_batch, tm=512):
    t, d = x.shape
    bpb = rows_per_batch // tm
    row = pl.BlockSpec((tm, d), lambda i: (i, 0))
    return pl.pallas_call(
        _lru_out_kernel,
        grid=(t // tm,),
        in_specs=[row, row, row,
                  pl.BlockSpec((1, 1, d), lambda i: (i // bpb, 0, 0)),
                  pl.BlockSpec((d, d), lambda i: (0, 0), pipeline_mode=pl.Buffered(1))],
        out_specs=row,
        out_shape=jax.ShapeDtypeStruct((t, d), F32),
        compiler_params=_params(("arbitrary",)),
        name="lru_out",
    )(z, y, x, g1, w16)


def _sc_out_kernel(bg_ref, cg_ref, v_ref, cw_ref, cb_ref, x_ref, g1_ref, w_ref, o_ref):
    taps = [cw_ref[k:k + 1, :] for k in range(cw_ref.shape[0])]
    y = _row_conv(cg_ref[...].astype(F32) * v_ref[...].astype(F32), taps, cb_ref[...], 1, GRID_W)
    p = bg_ref[...].astype(F32) * y
    out = jnp.dot(p.astype(BF16), w_ref[...], preferred_element_type=F32)
    o_ref[...] = x_ref[...] + g1_ref[0] * out


def _sc_out(z, conv_w, conv_b, x, g1, w16, rows_per_batch, tm=512):
    t, d = x.shape
    bpb = rows_per_batch // tm
    row = pl.BlockSpec((tm, d), lambda i: (i, 0))
    col = lambda c: pl.BlockSpec((tm, d), lambda i: (i, c))
    return pl.pallas_call(
        _sc_out_kernel,
        grid=(t // tm,),
        in_specs=[col(0), col(1), col(2),
                  pl.BlockSpec(conv_w.shape, lambda i: (0, 0)),
                  pl.BlockSpec((1, d), lambda i: (0, 0)),
                  row,
                  pl.BlockSpec((1, 1, d), lambda i: (i // bpb, 0, 0)),
                  pl.BlockSpec((d, d), lambda i: (0, 0), pipeline_mode=pl.Buffered(1))],
        out_specs=row,
        out_shape=jax.ShapeDtypeStruct((t, d), F32),
        compiler_params=_params(("arbitrary",)),
        name="sc_out",
    )(z, z, z, conv_w, conv_b.reshape(1, d), x, g1, w16)


def _top_rows(vals, idx_f, fill, n_out, emit):
    for r in range(n_out):
        m = jnp.max(vals, axis=0, keepdims=True)
        am = jnp.min(jnp.where(vals == m, idx_f, fill), axis=0, keepdims=True)
        emit(r, m, am)
        if r + 1 < n_out:
            vals = jnp.where(idx_f == am, -jnp.inf, vals)


def _route_kernel(q_ref, k_ref, ids_ref, gate_ref, sv_ref, si_ref, cv_ref, cand_ref, eid_ref, flat_ref):
    nk, dk = k_ref.shape[1], k_ref.shape[2]
    kk = PEER_TOPK
    n_cand = cand_ref.shape[0]
    tb = q_ref.shape[0]
    key_idx = lax.broadcasted_iota(jnp.int32, (nk, tb), 0).astype(F32)
    for p in range(2):
        s_t = lax.dot_general(k_ref[p], q_ref[:, p * dk:(p + 1) * dk], (((1,), (1,)), ((), ())),
                              precision=HIGHEST, preferred_element_type=F32)

        def emit(r, m, am, p=p):
            sv_ref[p, r:r + 1, :] = m
            si_ref[p, r:r + 1, :] = am

        _top_rows(s_t, key_idx, float(nk), kk, emit)

    sv1, si1 = sv_ref[1], si_ref[1]
    off = 0
    for i in range(kk):
        n_i = kk // (i + 1)
        cand_ref[off:off + n_i, :] = sv_ref[0, i:i + 1, :] + sv1[0:n_i, :]
        eid_ref[off:off + n_i, :] = si_ref[0, i:i + 1, :] * float(nk) + si1[0:n_i, :]
        flat_ref[off:off + n_i, :] = lax.broadcasted_iota(jnp.int32, (n_i, tb), 0).astype(F32) + float(i * kk)
        off += n_i
    cand_ref[off:n_cand, :] = jnp.full((n_cand - off, tb), -jnp.inf, F32)
    eid_ref[off:n_cand, :] = jnp.zeros((n_cand - off, tb), F32)
    flat_ref[off:n_cand, :] = jnp.full((n_cand - off, tb), float(kk * kk), F32)
    eid, flat = eid_ref[...], flat_ref[...]

    def emit2(r, m, am):
        e = jnp.max(jnp.where(flat == am, eid, -1.0), axis=0, keepdims=True)
        cv_ref[r:r + 1, :] = m
        ids_ref[r:r + 1, :] = e.astype(jnp.int32)

    _top_rows(cand_ref[...], flat, float(kk * kk), kk, emit2)
    cv = cv_ref[...]
    ex = jnp.exp(cv - jnp.max(cv, axis=0, keepdims=True))
    gate_ref[...] = ex / jnp.sum(ex, axis=0, keepdims=True)


def _peer_route(q, sub_keys, tb=1024):
    t = q.shape[0]
    nh, _, nk, dk = sub_keys.shape
    kk = PEER_TOPK
    n_cand = sum(kk // (i + 1) for i in range(kk))
    n_cand += (-n_cand) % SUBLANES
    o_spec = pl.BlockSpec((kk, tb), lambda i, h: (h, i))
    return pl.pallas_call(
        _route_kernel,
        grid=(t // tb, nh),
        in_specs=[pl.BlockSpec((tb, 2 * dk), lambda i, h: (i, h)),
                  pl.BlockSpec((None, 2, nk, dk), lambda i, h: (h, 0, 0, 0))],
        out_specs=[o_spec, o_spec],
        out_shape=[jax.ShapeDtypeStruct((nh * kk, t), jnp.int32),
                   jax.ShapeDtypeStruct((nh * kk, t), F32)],
        scratch_shapes=[pltpu.VMEM((2, kk, tb), F32), pltpu.VMEM((2, kk, tb), F32), pltpu.VMEM((kk, tb), F32)]
        + [pltpu.VMEM((n_cand, tb), F32)] * 3,
        compiler_params=_params(("arbitrary", "arbitrary")),
        name="peer_route",
    )(q, sub_keys)


def _pack_pair(x):
    half = x.shape[1] // 2
    bits = pltpu.bitcast(x.astype(BF16).astype(F32), jnp.uint32)
    return bits[:, half:] | (bits[:, :half] >> 16)


def _unpack_pair(w):
    return pltpu.bitcast(w << 16, F32), pltpu.bitcast(w & jnp.uint32(0xFFFF0000), F32)


def _pack_kernel(u_ref, v_ref, o_ref):
    half = u_ref.shape[1] // 2
    o_ref[:, 0, :half] = _pack_pair(u_ref[...])
    o_ref[:, 0, half:] = _pack_pair(v_ref[...])


def _pack_experts(u_all, v_all, layer, te=512):
    _, e, d = u_all.shape
    src = pl.BlockSpec((None, te, d), lambda i: (layer, i, 0))
    return pl.pallas_call(
        _pack_kernel,
        grid=(e // te,),
        in_specs=[src, src],
        out_specs=pl.BlockSpec((te, 1, d), lambda i: (i, 0, 0)),
        out_shape=jax.ShapeDtypeStruct((e, 1, d), jnp.uint32),
        compiler_params=_params(("arbitrary",)),
        name="pack_experts",
    )(u_all, v_all)


PEER_TOKENS_PER_STEP = 128
PEER_SLOTS = SUBLANES


def _peer_kernel(final_norm, head_ref, ids_ref, gate_ref, h_ref, x_ref, g2_ref, fin_ref, tab_ref, o_ref,
                 buf_ref, orow_ref, sem_ref):
    tb, d = h_ref.shape
    ne = PEER_PER_TOKEN
    nslots = buf_ref.shape[0]
    nword = d // (2 * LANES)
    ahead = nslots - 1
    step, last_step = pl.program_id(0), pl.num_programs(0) - 1

    def start_rows(src_ids, tok, slot, part=0, nparts=1):
        for k in range(part * ne // nparts, (part + 1) * ne // nparts):
            pltpu.make_async_copy(tab_ref.at[src_ids[tok, k]], buf_ref.at[slot, pl.ds(k, 1), :],
                                  sem_ref.at[slot]).start(priority=k % 2)

    def wait_token(slot):
        pltpu.make_async_copy(tab_ref.at[pl.ds(0, ne), 0], buf_ref.at[slot], sem_ref.at[slot]).wait()

    lane_tok = lax.broadcasted_iota(jnp.int32, (ne, tb), 1)

    def token(base, j, issue_next):
        wait_token(j)
        rows = pl.ds(base, nslots)
        acc = jnp.zeros((ne, LANES), F32)
        for c in range(nword):
            lo, hi = _unpack_pair(buf_ref[j, :, c * LANES:(c + 1) * LANES])
            acc = acc + lo * h_ref[rows, c * LANES:(c + 1) * LANES][j:j + 1, :]
            acc = acc + hi * h_ref[rows, (nword + c) * LANES:(nword + c + 1) * LANES][j:j + 1, :]
            issue_next(c, 2 * nword)
        act = _gelu_tanh(jnp.sum(acc, axis=1, keepdims=True))
        gcol = jnp.sum(jnp.where(lane_tok == base + j, gate_ref[...], 0.0), axis=1, keepdims=True)
        wb = jnp.broadcast_to(gcol * act, (ne, LANES))
        for c in range(nword):
            lo, hi = _unpack_pair(buf_ref[j, :, (nword + c) * LANES:(nword + c + 1) * LANES])
            orow_ref[j:j + 1, c * LANES:(c + 1) * LANES] = jnp.sum(lo * wb, axis=0, keepdims=True)
            orow_ref[j:j + 1, (nword + c) * LANES:(nword + c + 1) * LANES] = jnp.sum(hi * wb, axis=0, keepdims=True)
            issue_next(nword + c, 2 * nword)

    def finish_group(base):
        rows = pl.ds(base, nslots)
        y = x_ref[rows, :] + g2_ref[0] * orow_ref[...]
        if final_norm:
            y = (y * lax.rsqrt(jnp.mean(y * y, axis=-1, keepdims=True) + EPS)) * fin_ref[...]
        o_ref[rows, :] = y

    @pl.when(step == 0)
    def _():
        def first(s, carry):
            start_rows(head_ref, s, s)
            return carry
        lax.fori_loop(0, ahead, first, 0)

    def group(g, carry):
        base = pl.multiple_of(g * nslots, nslots)
        for j in range(nslots):
            token(base, j, functools.partial(start_rows, ids_ref, base + j, (j + ahead) % nslots))
        finish_group(base)
        return carry

    lax.fori_loop(0, tb // nslots, group, 0)

    @pl.when(step == last_step)
    def _():
        for s in range(ahead):
            wait_token(s)


def _peer_experts(ids_t, gates, h, x, g2, table, rows_per_batch, final_gamma, final_norm):
    t, d = x.shape
    tb = PEER_TOKENS_PER_STEP
    ne = PEER_PER_TOKEN
    bpb = rows_per_batch // tb
    ahead = PEER_SLOTS - 1
    ids_ahead = jnp.concatenate([ids_t[ahead:], jnp.zeros((ahead, ne), jnp.int32)], axis=0)
    row = pl.BlockSpec((tb, d), lambda i: (i, 0))
    return pl.pallas_call(
        functools.partial(_peer_kernel, final_norm),
        grid=(t // tb,),
        in_specs=[pl.BlockSpec((PEER_SLOTS, ne), lambda i: (0, 0), memory_space=pltpu.SMEM),
                  pl.BlockSpec((tb, ne), lambda i: (i, 0), memory_space=pltpu.SMEM),
                  pl.BlockSpec((ne, tb), lambda i: (0, i)),
                  row, row,
                  pl.BlockSpec((1, 1, d), lambda i: (i // bpb, 0, 0)),
                  pl.BlockSpec((1, d), lambda i: (0, 0)),
                  pl.BlockSpec(memory_space=pl.ANY)],
        out_specs=row,
        out_shape=jax.ShapeDtypeStruct((t, d), F32),
        scratch_shapes=[pltpu.VMEM((PEER_SLOTS, ne, d), jnp.uint32),
                        pltpu.VMEM((PEER_SLOTS, d), F32),
                        pltpu.SemaphoreType.DMA((PEER_SLOTS,))],
        compiler_params=_params(("arbitrary",)),
        name="peer_experts",
    )(ids_t[:PEER_SLOTS], ids_ahead, gates, h, x, g2, final_gamma.reshape(1, d), table)


def _peer_layer(x, mod, gamma, w_q, sub_keys, u_all, v_all, layer, seq, final_gamma, final_norm):
    bsz, d = mod.shape[0], x.shape[1]
    sh2 = mod[:, 3 * d:4 * d].reshape(bsz, 1, d)
    sc2 = mod[:, 4 * d:5 * d].reshape(bsz, 1, d)
    g2 = mod[:, 5 * d:6 * d].reshape(bsz, 1, d)
    q, h = _norm_mod_matmul(x, gamma, sh2, sc2, w_q.astype(BF16), seq, emit_h=True)
    ids, gates = _peer_route(q, sub_keys)
    table = _pack_experts(u_all, v_all, layer)
    return _peer_experts(ids.T, gates, h, x, g2, table, seq, final_gamma, final_norm)


def kernel(x, c, ctx, c_ctx, w_mod, b_mod, norm_mix_g, norm_ffn_g, norm_final_g, lru_w_in, lru_conv_w, lru_conv_b, lru_w_a, lru_b_a, lru_w_x, lru_b_x, lru_lambda, lru_w_out, sc_w_in, sc_conv_w, sc_conv_b, sc_w_out, peer_w_q, peer_sub_keys, peer_u, peer_v):
    bsz, seq, d = x.shape
    ctx_len = ctx.shape[1]
    xt = x.reshape(bsz * seq, d)

    c_rows = jnp.concatenate([c, c_ctx[None, :], jnp.zeros((SUBLANES - bsz - 1, d), F32)], axis=0)
    mods = [_mod_vectors(c_rows, w_mod, b_mod, i) for i in range(w_mod.shape[0])]

    mod = mods[0][:bsz]
    sh1, sc1, g1 = (mod[:, k * d:(k + 1) * d].reshape(bsz, 1, d) for k in range(3))
    w_in16 = lru_w_in[0].astype(BF16)
    w_a16, w_x16 = lru_w_a[0].astype(BF16), lru_w_x[0].astype(BF16)
    z = _norm_mod_matmul(xt, norm_mix_g[0], sh1, sc1, w_in16, seq, out_dtype=BF16)

    mod_c = mods[0][bsz:bsz + 1]
    sh_c = jnp.broadcast_to(mod_c[:, 0:d].reshape(1, 1, d), (bsz, 1, d))
    sc_c = jnp.broadcast_to(mod_c[:, d:2 * d].reshape(1, 1, d), (bsz, 1, d))
    z_ctx = _norm_mod_matmul(ctx.reshape(bsz * ctx_len, d), norm_mix_g[0], sh_c, sc_c, w_in16, ctx_len,
                             out_dtype=BF16)
    coef_args = (1, lru_conv_w[0], lru_conv_b[0], w_a16, lru_b_a[0], w_x16, lru_b_x[0], lru_lambda[0])
    cf = [t.reshape(bsz, ctx_len, d) for t in _lru_coeffs(z_ctx, *coef_args, ctx_len, ctx_len)]
    zero_state = jnp.zeros((bsz, 1, d), F32)
    hc_f = _linear_scan(cf[0], cf[1], zero_state, False)
    hc_r = _linear_scan(cf[2], cf[3], zero_state, True)

    lf = [t.reshape(bsz, seq, d) for t in _lru_coeffs(z, *coef_args, GRID_W, 256)]
    h_f = _linear_scan(lf[0], lf[1], hc_f[:, ctx_len - 1:ctx_len, :], False)
    y = _linear_scan(lf[2], lf[3], hc_r[:, 0:1, :], True, prev=h_f)
    xt = _lru_out(z, y.reshape(bsz * seq, d), xt, g1, lru_w_out[0].astype(BF16), seq)
    xt = _peer_layer(xt, mod, norm_ffn_g[0], peer_w_q[0], peer_sub_keys[0], peer_u, peer_v, 0, seq,
                     norm_final_g, False)

    mod = mods[1][:bsz]
    sh1, sc1, g1 = (mod[:, k * d:(k + 1) * d].reshape(bsz, 1, d) for k in range(3))
    z = _norm_mod_matmul(xt, norm_mix_g[1], sh1, sc1, sc_w_in[0].astype(BF16), seq, out_dtype=BF16)
    xt = _sc_out(z, sc_conv_w[0], sc_conv_b[0], xt, g1, sc_w_out[0].astype(BF16), seq)
    xt = _peer_layer(xt, mod, norm_ffn_g[1], peer_w_q[1], peer_sub_keys[1], peer_u, peer_v, 1, seq,
                     norm_final_g, True)
    return xt.reshape(bsz, seq, d)
```

```python
import functools

import jax
import jax.numpy as jnp
from jax import lax
from jax.experimental import pallas as pl
from jax.experimental.pallas import tpu as pltpu

EPS = 1e-6
GRID_W = 64
LRU_HEADS = 16
LRU_C = 8.0
PEER_HEADS = 8
PEER_N_KEYS = 128
PEER_TOPK = 16
PEER_PER_TOKEN = PEER_HEADS * PEER_TOPK

LANES = 128
SUBLANES = 8
VMEM_LIMIT = 48 * 1024 * 1024

F32 = jnp.float32
BF16 = jnp.bfloat16
HIGHEST = lax.Precision.HIGHEST


def _params(sem):
    return pltpu.CompilerParams(dimension_semantics=sem, vmem_limit_bytes=VMEM_LIMIT)


def _gelu_tanh(x):
    c = 0.7978845608028654
    return x * (0.5 * (1.0 + jnp.tanh(c * (x + 0.044715 * (x * x * x)))))


def _neg_expm1_2x(half_z, exp_z):
    return -jnp.tanh(half_z) * (exp_z + 1.0)


def _sigmoid(x):
    return 0.5 * (jnp.tanh(0.5 * x) + 1.0)


def _mod_kernel(c_ref, w_ref, b_ref, o_ref):
    c = c_ref[...]
    s = c * jax.nn.sigmoid(c)
    o_ref[...] = jnp.dot(s, w_ref[...], preferred_element_type=F32, precision=HIGHEST) + b_ref[...]


def _mod_vectors(c_rows, w_all, b_all, layer):
    nl, d, n = w_all.shape
    tn = 1024
    return pl.pallas_call(
        _mod_kernel,
        grid=(n // tn,),
        in_specs=[pl.BlockSpec((SUBLANES, d), lambda j: (0, 0)),
                  pl.BlockSpec((None, d, tn), lambda j: (layer, 0, j)),
                  pl.BlockSpec((None, 1, tn), lambda j: (layer, 0, j))],
        out_specs=pl.BlockSpec((SUBLANES, tn), lambda j: (0, j)),
        out_shape=jax.ShapeDtypeStruct((SUBLANES, n), F32),
        compiler_params=_params(("arbitrary",)),
        name="mod_vectors",
    )(c_rows, w_all, b_all.reshape(nl, 1, n))


def _norm_modulate(x, gamma, shift, scale):
    y = (x * lax.rsqrt(jnp.mean(x * x, axis=-1, keepdims=True) + EPS)) * gamma
    return y * (1.0 + scale) + shift


def _nmm_kernel(x_ref, g_ref, sh_ref, sc_ref, w_ref, o_ref, hn_ref):
    @pl.when(pl.program_id(1) == 0)
    def _():
        hn_ref[...] = _norm_modulate(x_ref[...], g_ref[...], sh_ref[0], sc_ref[0]).astype(BF16)

    o_ref[...] = jnp.dot(hn_ref[...], w_ref[...], preferred_element_type=F32).astype(o_ref.dtype)


def _norm_mod_matmul(x, gamma, shift, scale, w_bf16, rows_per_batch, out_dtype=F32):
    t, d = x.shape
    n = w_bf16.shape[1]
    tn = 1024
    tm = min(1024, rows_per_batch)
    bpb = rows_per_batch // tm
    out_shape = [jax.ShapeDtypeStruct((t, n), out_dtype)]
    out_specs = [pl.BlockSpec((tm, tn), lambda i, j: (i, j))]
    res = pl.pallas_call(
        _nmm_kernel,
        grid=(t // tm, n // tn),
        in_specs=[pl.BlockSpec((tm, d), lambda i, j: (i, 0)),
                  pl.BlockSpec((1, d), lambda i, j: (0, 0)),
                  pl.BlockSpec((1, 1, d), lambda i, j: (i // bpb, 0, 0)),
                  pl.BlockSpec((1, 1, d), lambda i, j: (i // bpb, 0, 0)),
                  pl.BlockSpec((d, tn), lambda i, j: (0, j))],
        out_specs=out_specs,
        out_shape=out_shape,
        scratch_shapes=[pltpu.VMEM((tm, d), BF16)],
        compiler_params=_params(("arbitrary", "arbitrary")),
        name="norm_mod_matmul",
    )(x, gamma.reshape(1, d), shift, scale, w_bf16)
    return res[0]


def _row_conv(u, taps, bias, pad_left, row_len):
    tm = u.shape[0]
    pos = lax.broadcasted_iota(jnp.int32, (tm, 1), 0) % row_len
    y = None
    for k in range(len(taps)):
        off = k - pad_left
        if off == 0:
            term = u
        else:
            rolled = pltpu.roll(u, (-off) % tm, axis=0)
            valid = jnp.logical_and(pos + off >= 0, pos + off < row_len)
            term = jnp.where(valid, rolled, 0.0)
        y = (bias + taps[k] * term) if y is None else (y + taps[k] * term)
    return y


def _lru_coef_kernel(row_len, xb_ref, cw_ref, cb_ref, wa_ref, ba_ref, wx_ref, bx_ref, lam_ref,
                     af_ref, bf_ref, ar_ref, br_ref):
    taps = [cw_ref[k:k + 1, :] for k in range(cw_ref.shape[0])]
    xc = _row_conv(xb_ref[...].astype(F32), taps, cb_ref[...], 2, row_len)
    d = xc.shape[1]
    hd = d // LRU_HEADS
    outs = ((af_ref, bf_ref), (ar_ref, br_ref))
    for dr in range(2):
        lam = lam_ref[dr:dr + 1, :]
        z = -lam
        nsp = (-LRU_C) * (jnp.maximum(z, 0.0) + jnp.log1p(jnp.exp(-jnp.abs(z))))
        a_ref, b_ref = outs[dr]
        for h in range(LRU_HEADS):
            sl = slice(h * hd, (h + 1) * hd)
            xh = xc[:, sl]
            xh16 = xh.astype(BF16)
            r = _sigmoid(jnp.dot(xh16, wa_ref[dr, h], preferred_element_type=F32) + ba_ref[dr:dr + 1, sl])
            i = _sigmoid(jnp.dot(xh16, wx_ref[dr, h], preferred_element_type=F32) + bx_ref[dr:dr + 1, sl])
            log_a = r * nsp[:, sl]
            a = jnp.exp(log_a)
            a_ref[:, sl] = a
            b_ref[:, sl] = jnp.sqrt(_neg_expm1_2x(log_a, a * a)) * (i * xh)


def _lru_coeffs(z, col_block, conv_w, conv_b, w_a16, b_a, w_x16, b_x, lam, row_len, tm):
    t = z.shape[0]
    d = conv_w.shape[1]
    full = lambda *s: pl.BlockSpec(s, lambda i: (0,) * len(s))
    o_spec = pl.BlockSpec((tm, d), lambda i: (i, 0))
    o_shape = jax.ShapeDtypeStruct((t, d), F32)
    return pl.pallas_call(
        functools.partial(_lru_coef_kernel, row_len),
        grid=(t // tm,),
        in_specs=[pl.BlockSpec((tm, d), lambda i: (i, col_block)),
                  full(*conv_w.shape), full(1, d),
                  full(*w_a16.shape), full(2, d), full(*w_x16.shape), full(2, d), full(2, d)],
        out_specs=[o_spec] * 4,
        out_shape=[o_shape] * 4,
        compiler_params=_params(("arbitrary",)),
        name="lru_coeffs",
    )(z, conv_w, conv_b.reshape(1, d), w_a16, b_a, w_x16, b_x, lam)


def _scan_kernel(reverse, add_prev, a_ref, b_ref, h0_ref, *rest):
    if add_prev:
        y_ref, o_ref, st_ref = rest
    else:
        o_ref, st_ref = rest
    ts, cw = a_ref.shape
    ng = ts // SUBLANES

    @pl.when(pl.program_id(2) == 0)
    def _():
        st_ref[...] = jnp.broadcast_to(h0_ref[...], st_ref.shape)

    row = lax.broadcasted_iota(jnp.int32, (SUBLANES, cw), 0)

    def body(gi, h):
        g = (ng - 1 - gi) if reverse else gi
        off = pl.multiple_of(g * SUBLANES, SUBLANES)
        a = a_ref[pl.ds(off, SUBLANES), :]
        b = b_ref[pl.ds(off, SUBLANES), :]
        for s in (1, 2, 4):
            if reverse:
                shift, m = SUBLANES - s, row < SUBLANES - s
            else:
                shift, m = s, row >= s
            a_s = jnp.where(m, pltpu.roll(a, shift, axis=0), 1.0)
            b_s = jnp.where(m, pltpu.roll(b, shift, axis=0), 0.0)
            b = a * b_s + b
            a = a * a_s
        hh = a * h + b
        if add_prev:
            o_ref[pl.ds(off, SUBLANES), :] = hh + y_ref[pl.ds(off, SUBLANES), :]
        else:
            o_ref[pl.ds(off, SUBLANES), :] = hh
        last = hh[0:1, :] if reverse else hh[SUBLANES - 1:SUBLANES, :]
        return jnp.broadcast_to(last, (SUBLANES, cw))

    st_ref[...] = lax.fori_loop(0, ng, body, st_ref[...], unroll=2)


def _linear_scan(a, b, h0, reverse, prev=None, ts=1024, cw=512):
    bsz, s, c = a.shape
    ts = min(ts, s)
    ns = s // ts
    tmap = (lambda bb, cc, j: (bb, ns - 1 - j, cc)) if reverse else (lambda bb, cc, j: (bb, j, cc))
    blk = pl.BlockSpec((None, ts, cw), tmap)
    ins = [a, b, h0]
    in_specs = [blk, blk, pl.BlockSpec((None, 1, cw), lambda bb, cc, j: (bb, 0, cc))]
    if prev is not None:
        ins.append(prev)
        in_specs.append(blk)
    return pl.pallas_call(
        functools.partial(_scan_kernel, reverse, prev is not None),
        grid=(bsz, c // cw, ns),
        in_specs=in_specs,
        out_specs=blk,
        out_shape=jax.ShapeDtypeStruct((bsz, s, c), F32),
        scratch_shapes=[pltpu.VMEM((SUBLANES, cw), F32)],
        compiler_params=_params(("arbitrary", "arbitrary", "arbitrary")),
        name="linear_scan",
    )(*ins)


def _lru_out_kernel(gate_ref, y_ref, x_ref, g1_ref, w_ref, o_ref):
    p = _gelu_tanh(gate_ref[...].astype(F32)) * y_ref[...]
    out = jnp.dot(p.astype(BF16), w_ref[...], preferred_element_type=F32)
    o_ref[...] = x_ref[...] + g1_ref[0] * out


def _lru_out(z, y, x, g1, w16, rows_per_batch, tm=512):
    t, d = x.shape
    bpb = rows_per_batch // tm
    row = pl.BlockSpec((tm, d), lambda i: (i, 0))
    return pl.pallas_call(
        _lru_out_kernel,
        grid=(t // tm,),
        in_specs=[row, row, row,
                  pl.BlockSpec((1, 1, d), lambda i: (i // bpb, 0, 0)),
                  pl.BlockSpec((d, d), lambda i: (0, 0), pipeline_mode=pl.Buffered(1))],
        out_specs=row,
        out_shape=jax.ShapeDtypeStruct((t, d), F32),
        compiler_params=_params(("arbitrary",)),
        name="lru_out",
    )(z, y, x, g1, w16)


def _sc_out_kernel(bg_ref, cg_ref, v_ref, cw_ref, cb_ref, x_ref, g1_ref, w_ref, o_ref):
    taps = [cw_ref[k:k + 1, :] for k in range(cw_ref.shape[0])]
    y = _row_conv(cg_ref[...].astype(F32) * v_ref[...].astype(F32), taps, cb_ref[...], 1, GRID_W)
    p = bg_ref[...].astype(F32) * y
    out = jnp.dot(p.astype(BF16), w_ref[...], preferred_element_type=F32)
    o_ref[...] = x_ref[...] + g1_ref[0] * out


def _sc_out(z, conv_w, conv_b, x, g1, w16, rows_per_batch, tm=512):
    t, d = x.shape
    bpb = rows_per_batch // tm
    row = pl.BlockSpec((tm, d), lambda i: (i, 0))
    col = lambda c: pl.BlockSpec((tm, d), lambda i: (i, c))
    return pl.pallas_call(
        _sc_out_kernel,
        grid=(t // tm,),
        in_specs=[col(0), col(1), col(2),
                  pl.BlockSpec(conv_w.shape, lambda i: (0, 0)),
                  pl.BlockSpec((1, d), lambda i: (0, 0)),
                  row,
                  pl.BlockSpec((1, 1, d), lambda i: (i // bpb, 0, 0)),
                  pl.BlockSpec((d, d), lambda i: (0, 0), pipeline_mode=pl.Buffered(1))],
        out_specs=row,
        out_shape=jax.ShapeDtypeStruct((t, d), F32),
        compiler_params=_params(("arbitrary",)),
        name="sc_out",
    )(z, z, z, conv_w, conv_b.reshape(1, d), x, g1, w16)


def _top_rows(vals, idx_f, fill, n_out, emit):
    for r in range(n_out):
        m = jnp.max(vals, axis=0, keepdims=True)
        am = jnp.min(jnp.where(vals == m, idx_f, fill), axis=0, keepdims=True)
        emit(r, m, am)
        if r + 1 < n_out:
            vals = jnp.where(idx_f == am, -jnp.inf, vals)


def _route_kernel(q_ref, k_ref, ids_ref, gate_ref, sv_ref, si_ref, cv_ref, cand_ref, eid_ref, flat_ref):
    nk, dk = k_ref.shape[1], k_ref.shape[2]
    kk = PEER_TOPK
    n_cand = cand_ref.shape[0]
    tb = q_ref.shape[0]
    key_idx = lax.broadcasted_iota(jnp.int32, (nk, tb), 0).astype(F32)
    for p in range(2):
        s_t = lax.dot_general(k_ref[p], q_ref[:, p * dk:(p + 1) * dk], (((1,), (1,)), ((), ())),
                              precision=HIGHEST, preferred_element_type=F32)

        def emit(r, m, am, p=p):
            sv_ref[p, r:r + 1, :] = m
            si_ref[p, r:r + 1, :] = am

        _top_rows(s_t, key_idx, float(nk), kk, emit)

    sv1, si1 = sv_ref[1], si_ref[1]
    off = 0
    for i in range(kk):
        n_i = kk // (i + 1)
        cand_ref[off:off + n_i, :] = sv_ref[0, i:i + 1, :] + sv1[0:n_i, :]
        eid_ref[off:off + n_i, :] = si_ref[0, i:i + 1, :] * float(nk) + si1[0:n_i, :]
        flat_ref[off:off + n_i, :] = lax.broadcasted_iota(jnp.int32, (n_i, tb), 0).astype(F32) + float(i * kk)
        off += n_i
    cand_ref[off:n_cand, :] = jnp.full((n_cand - off, tb), -jnp.inf, F32)
    eid_ref[off:n_cand, :] = jnp.zeros((n_cand - off, tb), F32)
    flat_ref[off:n_cand, :] = jnp.full((n_cand - off, tb), float(kk * kk), F32)
    eid, flat = eid_ref[...], flat_ref[...]

    def emit2(r, m, am):
        e = jnp.max(jnp.where(flat == am, eid, -1.0), axis=0, keepdims=True)
        cv_ref[r:r + 1, :] = m
        ids_ref[r:r + 1, :] = e.astype(jnp.int32)

    _top_rows(cand_ref[...], flat, float(kk * kk), kk, emit2)
    cv = cv_ref[...]
    ex = jnp.exp(cv - jnp.max(cv, axis=0, keepdims=True))
    gate_ref[...] = ex / jnp.sum(ex, axis=0, keepdims=True)


def _peer_route(q, sub_keys, tb=1024):
    t = q.shape[0]
    nh, _, nk, dk = sub_keys.shape
    kk = PEER_TOPK
    n_cand = sum(kk // (i + 1) for i in range(kk))
    n_cand += (-n_cand) % SUBLANES
    o_spec = pl.BlockSpec((kk, tb), lambda i, h: (h, i))
    return pl.pallas_call(
        _route_kernel,
        grid=(t // tb, nh),
        in_specs=[pl.BlockSpec((tb, 2 * dk), lambda i, h: (i, h)),
                  pl.BlockSpec((None, 2, nk, dk), lambda i, h: (h, 0, 0, 0))],
        out_specs=[o_spec, o_spec],
        out_shape=[jax.ShapeDtypeStruct((nh * kk, t), jnp.int32),
                   jax.ShapeDtypeStruct((nh * kk, t), F32)],
        scratch_shapes=[pltpu.VMEM((2, kk, tb), F32), pltpu.VMEM((2, kk, tb), F32), pltpu.VMEM((kk, tb), F32)]
        + [pltpu.VMEM((n_cand, tb), F32)] * 3,
        compiler_params=_params(("arbitrary", "arbitrary")),
        name="peer_route",
    )(q, sub_keys)


def _pack_pair(x):
    half = x.shape[1] // 2
    bits = pltpu.bitcast(x.astype(BF16).astype(F32), jnp.uint32)
    return bits[:, half:] | (bits[:, :half] >> 16)


def _unpack_pair(w):
    return pltpu.bitcast(w << 16, F32), pltpu.bitcast(w & jnp.uint32(0xFFFF0000), F32)


def _pack_kernel(u_ref, v_ref, o_ref):
    half = u_ref.shape[1] // 2
    o_ref[:, 0, :half] = _pack_pair(u_ref[...])
    o_ref[:, 0, half:] = _pack_pair(v_ref[...])


def _pack_experts(u_all, v_all, layer, te=512):
    _, e, d = u_all.shape
    src = pl.BlockSpec((None, te, d), lambda i: (layer, i, 0))
    return pl.pallas_call(
        _pack_kernel,
        grid=(e // te,),
        in_specs=[src, src],
        out_specs=pl.BlockSpec((te, 1, d), lambda i: (i, 0, 0)),
        out_shape=jax.ShapeDtypeStruct((e, 1, d), jnp.uint32),
        compiler_params=_params(("arbitrary",)),
        name="pack_experts",
    )(u_all, v_all)


PEER_TOKENS_PER_STEP = 128
PEER_SLOTS = SUBLANES


def _peer_kernel(final_norm, head_ref, ids_ref, gate_ref, x_ref, ng_ref, sh_ref, sc_ref, g2_ref, fin_ref, tab_ref,
                 o_ref, buf_ref, hrow_ref, orow_ref, sem_ref):
    tb, d = x_ref.shape
    ne = PEER_PER_TOKEN
    nslots = buf_ref.shape[0]
    nword = d // (2 * LANES)
    ahead = nslots - 1
    step, last_step = pl.program_id(0), pl.num_programs(0) - 1

    def start_rows(src_ids, tok, slot, part=0, nparts=1):
        for k in range(part * ne // nparts, (part + 1) * ne // nparts):
            pltpu.make_async_copy(tab_ref.at[src_ids[tok, k]], buf_ref.at[slot, pl.ds(k, 1), :],
                                  sem_ref.at[slot]).start(priority=k % 2)

    def wait_token(slot):
        pltpu.make_async_copy(tab_ref.at[pl.ds(0, ne), 0], buf_ref.at[slot], sem_ref.at[slot]).wait()

    lane_tok = lax.broadcasted_iota(jnp.int32, (ne, tb), 1)

    def token(base, j, issue_next):
        wait_token(j)
        acc = jnp.zeros((ne, LANES), F32)
        for c in range(nword):
            lo, hi = _unpack_pair(buf_ref[j, :, c * LANES:(c + 1) * LANES])
            acc = acc + lo * hrow_ref[j:j + 1, c * LANES:(c + 1) * LANES]
            acc = acc + hi * hrow_ref[j:j + 1, (nword + c) * LANES:(nword + c + 1) * LANES]
            issue_next(c, 2 * nword)
        act = _gelu_tanh(jnp.sum(acc, axis=1, keepdims=True))
        gcol = jnp.sum(jnp.where(lane_tok == base + j, gate_ref[...], 0.0), axis=1, keepdims=True)
        wb = jnp.broadcast_to(gcol * act, (ne, LANES))
        for c in range(nword):
            lo, hi = _unpack_pair(buf_ref[j, :, (nword + c) * LANES:(nword + c + 1) * LANES])
            orow_ref[j:j + 1, c * LANES:(c + 1) * LANES] = jnp.sum(lo * wb, axis=0, keepdims=True)
            orow_ref[j:j + 1, (nword + c) * LANES:(nword + c + 1) * LANES] = jnp.sum(hi * wb, axis=0, keepdims=True)
            issue_next(nword + c, 2 * nword)

    def finish_group(base):
        rows = pl.ds(base, nslots)
        y = x_ref[rows, :] + g2_ref[0] * orow_ref[...]
        if final_norm:
            y = (y * lax.rsqrt(jnp.mean(y * y, axis=-1, keepdims=True) + EPS)) * fin_ref[...]
        o_ref[rows, :] = y

    @pl.when(step == 0)
    def _():
        def first(s, carry):
            start_rows(head_ref, s, s)
            return carry
        lax.fori_loop(0, ahead, first, 0)

    def group(g, carry):
        base = pl.multiple_of(g * nslots, nslots)
        hrow_ref[...] = _norm_modulate(x_ref[pl.ds(base, nslots), :], ng_ref[...], sh_ref[0], sc_ref[0])
        for j in range(nslots):
            token(base, j, functools.partial(start_rows, ids_ref, base + j, (j + ahead) % nslots))
        finish_group(base)
        return carry

    lax.fori_loop(0, tb // nslots, group, 0)

    @pl.when(step == last_step)
    def _():
        for s in range(ahead):
            wait_token(s)


def _peer_experts(ids_t, gates, x, gamma, shift, scale, g2, table, rows_per_batch, final_gamma, final_norm):
    t, d = x.shape
    tb = PEER_TOKENS_PER_STEP
    ne = PEER_PER_TOKEN
    bpb = rows_per_batch // tb
    ahead = PEER_SLOTS - 1
    ids_ahead = jnp.concatenate([ids_t[ahead:], jnp.zeros((ahead, ne), jnp.int32)], axis=0)
    row = pl.BlockSpec((tb, d), lambda i: (i, 0))
    per_batch = pl.BlockSpec((1, 1, d), lambda i: (i // bpb, 0, 0))
    vec = pl.BlockSpec((1, d), lambda i: (0, 0))
    return pl.pallas_call(
        functools.partial(_peer_kernel, final_norm),
        grid=(t // tb,),
        in_specs=[pl.BlockSpec((PEER_SLOTS, ne), lambda i: (0, 0), memory_space=pltpu.SMEM),
                  pl.BlockSpec((tb, ne), lambda i: (i, 0), memory_space=pltpu.SMEM),
                  pl.BlockSpec((ne, tb), lambda i: (0, i)),
                  row, vec, per_batch, per_batch, per_batch, vec,
                  pl.BlockSpec(memory_space=pl.ANY)],
        out_specs=row,
        out_shape=jax.ShapeDtypeStruct((t, d), F32),
        scratch_shapes=[pltpu.VMEM((PEER_SLOTS, ne, d), jnp.uint32),
                        pltpu.VMEM((PEER_SLOTS, d), F32),
                        pltpu.VMEM((PEER_SLOTS, d), F32),
                        pltpu.SemaphoreType.DMA((PEER_SLOTS,))],
        compiler_params=_params(("arbitrary",)),
        name="peer_experts",
    )(ids_t[:PEER_SLOTS], ids_ahead, gates, x, gamma.reshape(1, d), shift, scale, g2, final_gamma.reshape(1, d),
      table)


def _peer_layer(x, mod, gamma, w_q, sub_keys, u_all, v_all, layer, seq, final_gamma, final_norm):
    bsz, d = mod.shape[0], x.shape[1]
    sh2 = mod[:, 3 * d:4 * d].reshape(bsz, 1, d)
    sc2 = mod[:, 4 * d:5 * d].reshape(bsz, 1, d)
    g2 = mod[:, 5 * d:6 * d].reshape(bsz, 1, d)
    q = _norm_mod_matmul(x, gamma, sh2, sc2, w_q.astype(BF16), seq)
    ids, gates = _peer_route(q, sub_keys)
    table = _pack_experts(u_all, v_all, layer)
    return _peer_experts(ids.T, gates, x, gamma, sh2, sc2, g2, table, seq, final_gamma, final_norm)


def kernel(x, c, ctx, c_ctx, w_mod, b_mod, norm_mix_g, norm_ffn_g, norm_final_g, lru_w_in, lru_conv_w, lru_conv_b, lru_w_a, lru_b_a, lru_w_x, lru_b_x, lru_lambda, lru_w_out, sc_w_in, sc_conv_w, sc_conv_b, sc_w_out, peer_w_q, peer_sub_keys, peer_u, peer_v):
    bsz, seq, d = x.shape
    ctx_len = ctx.shape[1]
    xt = x.reshape(bsz * seq, d)

    c_rows = jnp.concatenate([c, c_ctx[None, :], jnp.zeros((SUBLANES - bsz - 1, d), F32)], axis=0)
    mods = [_mod_vectors(c_rows, w_mod, b_mod, i) for i in range(w_mod.shape[0])]

    mod = mods[0][:bsz]
    sh1, sc1, g1 = (mod[:, k * d:(k + 1) * d].reshape(bsz, 1, d) for k in range(3))
    w_in16 = lru_w_in[0].astype(BF16)
    w_a16, w_x16 = lru_w_a[0].astype(BF16), lru_w_x[0].astype(BF16)
    z = _norm_mod_matmul(xt, norm_mix_g[0], sh1, sc1, w_in16, seq, out_dtype=BF16)

    mod_c = mods[0][bsz:bsz + 1]
    sh_c = jnp.broadcast_to(mod_c[:, 0:d].reshape(1, 1, d), (bsz, 1, d))
    sc_c = jnp.broadcast_to(mod_c[:, d:2 * d].reshape(1, 1, d), (bsz, 1, d))
    z_ctx = _norm_mod_matmul(ctx.reshape(bsz * ctx_len, d), norm_mix_g[0], sh_c, sc_c, w_in16, ctx_len,
                             out_dtype=BF16)
    coef_args = (1, lru_conv_w[0], lru_conv_b[0], w_a16, lru_b_a[0], w_x16, lru_b_x[0], lru_lambda[0])
    cf = [t.reshape(bsz, ctx_len, d) for t in _lru_coeffs(z_ctx, *coef_args, ctx_len, ctx_len)]
    zero_state = jnp.zeros((bsz, 1, d), F32)
    hc_f = _linear_scan(cf[0], cf[1], zero_state, False)
    hc_r = _linear_scan(cf[2], cf[3], zero_state, True)

    lf = [t.reshape(bsz, seq, d) for t in _lru_coeffs(z, *coef_args, GRID_W, 256)]
    h_f = _linear_scan(lf[0], lf[1], hc_f[:, ctx_len - 1:ctx_len, :], False)
    y = _linear_scan(lf[2], lf[3], hc_r[:, 0:1, :], True, prev=h_f)
    xt = _lru_out(z, y.reshape(bsz * seq, d), xt, g1, lru_w_out[0].astype(BF16), seq)
    xt = _peer_layer(xt, mod, norm_ffn_g[0], peer_w_q[0], peer_sub_keys[0], peer_u, peer_v, 0, seq,
                     norm_final_g, False)

    mod = mods[1][:bsz]
    sh1, sc1, g1 = (mod[:, k * d:(k + 1) * d].reshape(bsz, 1, d) for k in range(3))
    z = _norm_mod_matmul(xt, norm_mix_g[1], sh1, sc1, sc_w_in[0].astype(BF16), seq, out_dtype=BF16)
    xt = _sc_out(z, sc_conv_w[0], sc_conv_b[0], xt, g1, sc_w_out[0].astype(BF16), seq)
    xt = _peer_layer(xt, mod, norm_ffn_g[1], peer_w_q[1], peer_sub_keys[1], peer_u, peer_v, 1, seq,
                     norm_final_g, True)
    return xt.reshape(bsz, seq, d)
```
